```python
import math
import jax, jax.numpy as jnp
from jax import lax
import numpy as np

D_MODEL = 4096
BATCH = 1
SEQ = 16384
DEPTH = 4

N_MIXERS = 3
D_FF = 5632
NORM_EPS = 1e-6
ROPE_THETA = 10000.0
Q_BLOCK = 128
MASK_NEG = -1e30

NSA_HEADS = 32
NSA_KV_GROUPS = 4
NSA_HPG = NSA_HEADS // NSA_KV_GROUPS
NSA_HEAD_DIM = D_MODEL // NSA_HEADS
NSA_CMP_STRIDE = 16
NSA_CMP_LEN = 2 * NSA_CMP_STRIDE
NSA_SEL_BLOCK = 64
NSA_N_SEL = 16
NSA_WINDOW = 512
NSA_IN_WIDTH = NSA_HEADS * NSA_HEAD_DIM + 6 * NSA_KV_GROUPS * NSA_HEAD_DIM + 3 * NSA_HEADS
NSA_FORCE = 1e4

DSA_HEADS = 32
DSA_NOPE = 128
DSA_ROPE = 64
DSA_V = 128
DSA_LATENT = 512
DSA_IDX_HEADS = 32
DSA_IDX_DIM = 128
DSA_TOPK_MAX = 256
DSA_IN_WIDTH = DSA_HEADS * (DSA_NOPE + DSA_ROPE) + DSA_LATENT + DSA_ROPE + DSA_IDX_HEADS * DSA_IDX_DIM + DSA_IDX_DIM + DSA_IDX_HEADS

S5_GROUP = 16
S5_GROUPS = D_MODEL // S5_GROUP
S5_STATE = 64
S5_CHUNK = 128
S5_DT_MIN = 1e-3
S5_DT_MAX = 1e-1

XA_HEADS = 4
XA_HEAD_DIM = 128
MEM_LEN = 256

kernel_name = 'hybrid_nsa_dsa_s5_macaron_trunk'


def _count(kind):
    return len(range(kind, DEPTH, N_MIXERS))


def _split(a, sizes):
    out, o = [], 0
    for s in sizes:
        out.append(a[..., o:o + s])
        o += s
    return out


def rmsnorm(x, g):
    xf = x.astype(jnp.float32)
    y = xf * lax.rsqrt(jnp.mean(xf * xf, axis=-1, keepdims=True) + NORM_EPS)
    return (y * g.astype(jnp.float32)).astype(x.dtype)


def rope(x, pos):
    d = x.shape[-1]
    inv = ROPE_THETA ** (-jnp.arange(0, d, 2, dtype=jnp.float32) / d)
    ang = pos.astype(jnp.float32)[..., None] * inv
    ang = ang.reshape(ang.shape[:2] + (1,) * (x.ndim - 3) + (d // 2,))
    cos, sin = jnp.cos(ang), jnp.sin(ang)
    xf = x.astype(jnp.float32)
    x1, x2 = xf[..., :d // 2], xf[..., d // 2:]
    return jnp.concatenate([x1 * cos - x2 * sin, x1 * sin + x2 * cos], axis=-1).astype(x.dtype)


def masked_softmax(s, mask):
    s = jnp.where(mask, s.astype(jnp.float32), MASK_NEG)
    p = jax.nn.softmax(s, axis=-1)
    return jnp.where(mask, p, 0.0)


def swiglu(x, w_in, w_out):
    g, u = jnp.split(x @ w_in, 2, axis=-1)
    return (jax.nn.silu(g) * u) @ w_out


def to_blocks(a):
    b, t = a.shape[:2]
    return jnp.moveaxis(a.reshape((b, t // Q_BLOCK, Q_BLOCK) + a.shape[2:]), 1, 0)


def from_blocks(a):
    a = jnp.moveaxis(a, 0, 1)
    return a.reshape((a.shape[0], a.shape[1] * a.shape[2]) + a.shape[3:])


def nsa_compress(k, pos_emb, w1, w2):
    b, t, g, dh = k.shape
    nc = t // NSA_CMP_STRIDE
    kp = jnp.pad(k, ((0, 0), (0, NSA_CMP_STRIDE), (0, 0), (0, 0)))
    ch = kp.reshape(b, nc + 1, NSA_CMP_STRIDE, g, dh)
    blk = jnp.concatenate([ch[:, :-1], ch[:, 1:]], axis=2)
    blk = blk + pos_emb[None, None, :, None, :]
    blk = jnp.moveaxis(blk, 2, 3).reshape(b, nc, g, NSA_CMP_LEN * dh)
    return jax.nn.silu(blk @ w1) @ w2


def nsa_mixer(x, pos, w_in, cmp_pos, cmp_w1, cmp_w2, w_out):
    b, t, _ = x.shape
    g, hpg, dh = NSA_KV_GROUPS, NSA_HPG, NSA_HEAD_DIM
    kvw = g * dh
    q, kc, vc, ks, vs, kw, vw, gate = _split(x @ w_in, [NSA_HEADS * dh] + [kvw] * 6 + [3 * NSA_HEADS])
    q = q.reshape(b, t, g, hpg, dh)
    kc, vc, ks, vs, kw, vw = [a.reshape(b, t, g, dh) for a in (kc, vc, ks, vs, kw, vw)]
    qr = rope(q, pos)
    ks = rope(ks, pos)
    kw = rope(kw, pos)
    gate = jax.nn.sigmoid(gate.astype(jnp.float32)).reshape(b, t, g, hpg, 3)
    k_cmp = nsa_compress(kc, cmp_pos[0], cmp_w1[0], cmp_w2[0])
    v_cmp = nsa_compress(vc, cmp_pos[1], cmp_w1[1], cmp_w2[1])
    nc = t // NSA_CMP_STRIDE
    ns = t // NSA_SEL_BLOCK
    n_sel = min(NSA_N_SEL, ns)
    cmp_per_sel = NSA_SEL_BLOCK // NSA_CMP_STRIDE
    cmp_end = jnp.arange(nc) * NSA_CMP_STRIDE + NSA_CMP_LEN - 1
    sel_ids = jnp.arange(ns)
    ks_blk = jnp.moveaxis(ks.reshape(b, ns, NSA_SEL_BLOCK, g, dh), 3, 1)
    vs_blk = jnp.moveaxis(vs.reshape(b, ns, NSA_SEL_BLOCK, g, dh), 3, 1)
    kw_pad = jnp.pad(kw, ((0, 0), (NSA_WINDOW, 0), (0, 0), (0, 0)))
    vw_pad = jnp.pad(vw, ((0, 0), (NSA_WINDOW, 0), (0, 0), (0, 0)))
    wlen = NSA_WINDOW + Q_BLOCK
    scale = dh ** -0.5
    gather = jax.vmap(jax.vmap(lambda blocks, ids: blocks[ids]))

    def block(args):
        qb, qrb, gb, bi = args
        start = bi * Q_BLOCK
        tq = start + jnp.arange(Q_BLOCK)
        s = jnp.einsum('bqghd,bcgd->bgqhc', qb, k_cmp) * scale
        m = cmp_end[None, :] <= tq[:, None]
        p = masked_softmax(s, m[None, None, :, None, :])
        o_c = jnp.einsum('bgqhc,bcgd->bqghd', p.astype(v_cmp.dtype), v_cmp)
        imp = p.sum(axis=3).reshape(b, g, Q_BLOCK, ns, cmp_per_sel).sum(-1)
        cur = tq // NSA_SEL_BLOCK
        valid = sel_ids[None, :] * NSA_SEL_BLOCK <= tq[:, None]
        forced = (sel_ids[None, :] == 0) | (sel_ids[None, :] == cur[:, None]) | (sel_ids[None, :] == cur[:, None] - 1)
        imp = jnp.where(forced, NSA_FORCE, jnp.where(valid, imp, -1.0))
        _, idx = lax.top_k(imp, n_sel)
        k_sel = gather(ks_blk, idx).reshape(b, g, Q_BLOCK, n_sel * NSA_SEL_BLOCK, dh)
        v_sel = gather(vs_blk, idx).reshape(b, g, Q_BLOCK, n_sel * NSA_SEL_BLOCK, dh)
        pos_sel = (idx[..., None] * NSA_SEL_BLOCK + jnp.arange(NSA_SEL_BLOCK)).reshape(b, g, Q_BLOCK, n_sel * NSA_SEL_BLOCK)
        m = pos_sel <= tq[None, None, :, None]
        s = jnp.einsum('bqghd,bgqkd->bgqhk', qrb, k_sel) * scale
        p = masked_softmax(s, m[:, :, :, None, :])
        o_s = jnp.einsum('bgqhk,bgqkd->bqghd', p.astype(v_sel.dtype), v_sel)
        kwb = lax.dynamic_slice_in_dim(kw_pad, start, wlen, axis=1)
        vwb = lax.dynamic_slice_in_dim(vw_pad, start, wlen, axis=1)
        sk = start - NSA_WINDOW + jnp.arange(wlen)
        diff = tq[:, None] - sk[None, :]
        m = (diff >= 0) & (diff < NSA_WINDOW) & (sk[None, :] >= 0)
        s = jnp.einsum('bqghd,bkgd->bgqhk', qrb, kwb) * scale
        p = masked_softmax(s, m[None, None, :, None, :])
        o_w = jnp.einsum('bgqhk,bkgd->bqghd', p.astype(vwb.dtype), vwb)
        o = gb[..., 0:1] * o_c + gb[..., 1:2] * o_s + gb[..., 2:3] * o_w
        return o.astype(x.dtype).reshape(b, Q_BLOCK, NSA_HEADS * dh)

    out = lax.map(block, (to_blocks(q), to_blocks(qr), to_blocks(gate), jnp.arange(t // Q_BLOCK)))
    return from_blocks(out) @ w_out


def dsa_mixer(x, pos, w_in, kv_norm, w_uk, w_uv, w_out):
    b, t, _ = x.shape
    h = DSA_HEADS
    q_nope, q_pe, c_kv, k_pe, q_idx, k_idx, w_idx = _split(
        x @ w_in, [h * DSA_NOPE, h * DSA_ROPE, DSA_LATENT, DSA_ROPE, DSA_IDX_HEADS * DSA_IDX_DIM, DSA_IDX_DIM, DSA_IDX_HEADS])
    q_nope = q_nope.reshape(b, t, h, DSA_NOPE)
    q_pe = rope(q_pe.reshape(b, t, h, DSA_ROPE), pos)
    c_kv = rmsnorm(c_kv, kv_norm)
    k_pe = rope(k_pe, pos)
    q_idx = rope(q_idx.reshape(b, t, DSA_IDX_HEADS, DSA_IDX_DIM), pos)
    k_idx = rope(k_idx, pos)
    w_idx = w_idx * ((DSA_IDX_HEADS * DSA_IDX_DIM) ** -0.5)
    kv_lat = jnp.concatenate([c_kv, k_pe], axis=-1)
    top_k = min(DSA_TOPK_MAX, t // 4)
    scale = (DSA_NOPE + DSA_ROPE) ** -0.5
    s_all = jnp.arange(t)
    gather = jax.vmap(lambda cache, ids: cache[ids])

    def block(args):
        qn, qp, qi, wi, bi = args
        tq = bi * Q_BLOCK + jnp.arange(Q_BLOCK)
        causal = s_all[None, :] <= tq[:, None]
        logits = jnp.einsum('bqhd,bsd->bqhs', qi, k_idx)
        score = jnp.einsum('bqhs,bqh->bqs', jax.nn.relu(logits).astype(jnp.float32), wi.astype(jnp.float32))
        score = jnp.where(causal[None], score, -jnp.inf)
        _, idx = lax.top_k(score, top_k)
        sel = gather(kv_lat, idx)
        q_lat = jnp.einsum('bqhd,hcd->bqhc', qn, w_uk)
        qq = jnp.concatenate([q_lat, qp], axis=-1)
        s = jnp.einsum('bqhc,bqkc->bqhk', qq, sel) * scale
        m = idx <= tq[None, :, None]
        p = masked_softmax(s, m[:, :, None, :])
        o_lat = jnp.einsum('bqhk,bqkc->bqhc', p.astype(sel.dtype), sel[..., :DSA_LATENT])
        o = jnp.einsum('bqhc,hcv->bqhv', o_lat, w_uv)
        return o.reshape(b, Q_BLOCK, h * DSA_V)

    out = lax.map(block, (to_blocks(q_nope), to_blocks(q_pe), to_blocks(q_idx), to_blocks(w_idx), jnp.arange(t // Q_BLOCK)))
    return from_blocks(out) @ w_out


def _cscan_combine(e1, e2):
    a1r, a1i, b1r, b1i = e1
    a2r, a2i, b2r, b2i = e2
    return (a2r * a1r - a2i * a1i, a2r * a1i + a2i * a1r,
            a2r * b1r - a2i * b1i + b2r, a2r * b1i + a2i * b1r + b2i)


def s5_mixer(x, lam_re, lam_im, log_step, b_re, b_im, c_re, c_im, d_skip, w_glu):
    b, t, d = x.shape
    f32 = jnp.float32
    u = x.astype(f32).reshape(b, t, S5_GROUPS, S5_GROUP)
    dt = jnp.exp(log_step.astype(f32))[:, None]
    lr = jnp.minimum(lam_re.astype(f32), -1e-4)
    li = lam_im.astype(f32)
    mag = jnp.exp(lr * dt)
    a_re = mag * jnp.cos(li * dt)
    a_im = mag * jnp.sin(li * dt)
    den = lr * lr + li * li
    coef_re = ((a_re - 1.0) * lr + a_im * li) / den
    coef_im = (a_im * lr - (a_re - 1.0) * li) / den
    br, bi_ = b_re.astype(f32), b_im.astype(f32)
    bb_re = coef_re[..., None] * br - coef_im[..., None] * bi_
    bb_im = coef_re[..., None] * bi_ + coef_im[..., None] * br
    cr, ci = c_re.astype(f32), c_im.astype(f32)
    nch = t // S5_CHUNK
    u_ch = jnp.moveaxis(u.reshape(b, nch, S5_CHUNK, S5_GROUPS, S5_GROUP), 1, 0)
    a_re_b = jnp.broadcast_to(a_re, (b, S5_CHUNK, S5_GROUPS, S5_STATE))
    a_im_b = jnp.broadcast_to(a_im, (b, S5_CHUNK, S5_GROUPS, S5_STATE))

    def chunk_step(carry, uc):
        xr, xi = carry
        bur = jnp.einsum('btgi,gni->btgn', uc, bb_re)
        bui = jnp.einsum('btgi,gni->btgn', uc, bb_im)
        bur = bur.at[:, 0].add(a_re * xr - a_im * xi)
        bui = bui.at[:, 0].add(a_re * xi + a_im * xr)
        _, _, sr, si = lax.associative_scan(_cscan_combine, (a_re_b, a_im_b, bur, bui), axis=1)
        y = jnp.einsum('gon,btgn->btgo', cr, sr) - jnp.einsum('gon,btgn->btgo', ci, si)
        return (sr[:, -1], si[:, -1]), y

    init = (jnp.zeros((b, S5_GROUPS, S5_STATE), f32), jnp.zeros((b, S5_GROUPS, S5_STATE), f32))
    _, ys = lax.scan(chunk_step, init, u_ch)
    y = jnp.moveaxis(ys, 0, 1).reshape(b, t, d) + d_skip.astype(f32) * x.astype(f32)
    z = jax.nn.gelu(y).astype(x.dtype)
    h1, h2 = jnp.split(z @ w_glu, 2, axis=-1)
    return h1 * jax.nn.sigmoid(h2)


def cross_attn(x, mem, mem_norm, wq, wkv, wo):
    b, t, _ = x.shape
    q = (x @ wq).reshape(b, t, XA_HEADS, XA_HEAD_DIM)
    kv = (rmsnorm(mem, mem_norm) @ wkv).reshape(b, mem.shape[1], 2, XA_HEADS, XA_HEAD_DIM)
    k, v = kv[:, :, 0], kv[:, :, 1]
    s = jnp.einsum('bqhd,bmhd->bhqm', q, k) * (XA_HEAD_DIM ** -0.5)
    p = jax.nn.softmax(s.astype(jnp.float32), axis=-1).astype(v.dtype)
    o = jnp.einsum('bhqm,bmhd->bqhd', p, v).reshape(b, t, XA_HEADS * XA_HEAD_DIM)
    return o @ wo


def setup_inputs(seed: int = 0) -> dict:
    key = jax.random.key(seed)
    ks = jax.random.split(key, 32)
    n_a, n_b, n_c = _count(0), _count(1), _count(2)
    D, F = D_MODEL, D_FF
    nrm = jax.random.normal
    f32 = jnp.float32
    kv_nsa = NSA_KV_GROUPS * NSA_HEAD_DIM
    return {
        'x': nrm(ks[0], (BATCH, SEQ, D), f32),
        'mem': nrm(ks[1], (BATCH, MEM_LEN, D), f32),
        'positions': jnp.broadcast_to(jnp.arange(SEQ, dtype=jnp.int32), (BATCH, SEQ)),
        'norm_gains': 1.0 + 0.02 * nrm(ks[2], (DEPTH, 8, D), f32),
        'ffn_w_in': nrm(ks[3], (DEPTH, 2, D, 2 * F), f32) * D ** -0.5,
        'ffn_w_out': nrm(ks[4], (DEPTH, 2, F, D), f32) * F ** -0.5,
        'xa_mem_norm': 1.0 + 0.02 * nrm(ks[5], (DEPTH, D), f32),
        'xa_wq': nrm(ks[6], (DEPTH, D, XA_HEADS * XA_HEAD_DIM), f32) * D ** -0.5,
        'xa_wkv': nrm(ks[7], (DEPTH, D, 2 * XA_HEADS * XA_HEAD_DIM), f32) * D ** -0.5,
        'xa_wo': nrm(ks[8], (DEPTH, XA_HEADS * XA_HEAD_DIM, D), f32) * (XA_HEADS * XA_HEAD_DIM) ** -0.5,
        'nsa_w_in': nrm(ks[9], (n_a, D, NSA_IN_WIDTH), f32) * D ** -0.5,
        'nsa_cmp_pos': 0.02 * nrm(ks[10], (n_a, 2, NSA_CMP_LEN, NSA_HEAD_DIM), f32),
        'nsa_cmp_w1': nrm(ks[11], (n_a, 2, NSA_CMP_LEN * NSA_HEAD_DIM, NSA_HEAD_DIM), f32) * (NSA_CMP_LEN * NSA_HEAD_DIM) ** -0.5,
        'nsa_cmp_w2': nrm(ks[12], (n_a, 2, NSA_HEAD_DIM, NSA_HEAD_DIM), f32) * NSA_HEAD_DIM ** -0.5,
        'nsa_w_out': nrm(ks[13], (n_a, NSA_HEADS * NSA_HEAD_DIM, D), f32) * (NSA_HEADS * NSA_HEAD_DIM) ** -0.5,
        'dsa_w_in': nrm(ks[14], (n_b, D, DSA_IN_WIDTH), f32) * D ** -0.5,
        'dsa_kv_norm': 1.0 + 0.02 * nrm(ks[15], (n_b, DSA_LATENT), f32),
        'dsa_w_uk': nrm(ks[16], (n_b, DSA_HEADS, DSA_LATENT, DSA_NOPE), f32) * DSA_LATENT ** -0.5,
        'dsa_w_uv': nrm(ks[17], (n_b, DSA_HEADS, DSA_LATENT, DSA_V), f32) * DSA_LATENT ** -0.5,
        'dsa_w_out': nrm(ks[18], (n_b, DSA_HEADS * DSA_V, D), f32) * (DSA_HEADS * DSA_V) ** -0.5,
        's5_lam_re': -0.5 + 0.01 * nrm(ks[19], (n_c, S5_GROUPS, S5_STATE), f32),
        's5_lam_im': math.pi * jnp.arange(S5_STATE, dtype=f32) + 0.01 * nrm(ks[20], (n_c, S5_GROUPS, S5_STATE), f32),
        's5_log_step': jax.random.uniform(ks[21], (n_c, S5_GROUPS), f32, math.log(S5_DT_MIN), math.log(S5_DT_MAX)),
        's5_b_re': nrm(ks[22], (n_c, S5_GROUPS, S5_STATE, S5_GROUP), f32) * (2 * S5_GROUP) ** -0.5,
        's5_b_im': nrm(ks[23], (n_c, S5_GROUPS, S5_STATE, S5_GROUP), f32) * (2 * S5_GROUP) ** -0.5,
        's5_c_re': nrm(ks[24], (n_c, S5_GROUPS, S5_GROUP, S5_STATE), f32) * (2 * S5_STATE) ** -0.5,
        's5_c_im': nrm(ks[25], (n_c, S5_GROUPS, S5_GROUP, S5_STATE), f32) * (2 * S5_STATE) ** -0.5,
        's5_d': nrm(ks[26], (n_c, D), f32),
        's5_w_glu': nrm(ks[27], (n_c, D, 2 * D), f32) * D ** -0.5,
    }


def reference(x, mem, positions, norm_gains, ffn_w_in, ffn_w_out, xa_mem_norm, xa_wq, xa_wkv, xa_wo,
              nsa_w_in, nsa_cmp_pos, nsa_cmp_w1, nsa_cmp_w2, nsa_w_out,
              dsa_w_in, dsa_kv_norm, dsa_w_uk, dsa_w_uv, dsa_w_out,
              s5_lam_re, s5_lam_im, s5_log_step, s5_b_re, s5_b_im, s5_c_re, s5_c_im, s5_d, s5_w_glu):
    h = x
    for i in range(DEPTH):
        g = norm_gains[i]
        kind = i % N_MIXERS
        j = i // N_MIXERS
        h = h + 0.5 * rmsnorm(swiglu(rmsnorm(h, g[0]), ffn_w_in[i, 0], ffn_w_out[i, 0]), g[1])
        hn = rmsnorm(h, g[2])
        if kind == 0:
            y = nsa_mixer(hn, positions, nsa_w_in[j], nsa_cmp_pos[j], nsa_cmp_w1[j], nsa_cmp_w2[j], nsa_w_out[j])
        elif kind == 1:
            y = dsa_mixer(hn, positions, dsa_w_in[j], dsa_kv_norm[j], dsa_w_uk[j], dsa_w_uv[j], dsa_w_out[j])
        else:
            y = s5_mixer(hn, s5_lam_re[j], s5_lam_im[j], s5_log_step[j], s5_b_re[j], s5_b_im[j],
                         s5_c_re[j], s5_c_im[j], s5_d[j], s5_w_glu[j])
        h = h + rmsnorm(y, g[3])
        h = h + rmsnorm(cross_attn(rmsnorm(h, g[4]), mem, xa_mem_norm[i], xa_wq[i], xa_wkv[i], xa_wo[i]), g[5])
        h = h + 0.5 * rmsnorm(swiglu(rmsnorm(h, g[6]), ffn_w_in[i, 1], ffn_w_out[i, 1]), g[7])
    return h
```

```python
import functools
import math

import jax
import jax.numpy as jnp
from jax import lax
from jax.experimental import pallas as pl
from jax.experimental.pallas import tpu as pltpu

F32 = jnp.float32
BF16 = jnp.bfloat16

NORM_EPS = 1e-6
ROPE_THETA = 10000.0
MASK_NEG = -1e30
Q_BLOCK = 128

V7X_VMEM_LIMIT_BYTES = 56 * 1024 * 1024
LANES = 128


def _params(*sem):
    return pltpu.CompilerParams(dimension_semantics=sem, vmem_limit_bytes=V7X_VMEM_LIMIT_BYTES)


def _resident(shape):
    nd = len(shape)
    return pl.BlockSpec(shape, lambda *_: (0,) * nd, pipeline_mode=pl.Buffered(1))


def _mm_kernel(a_ref, b_ref, o_ref):
    o_ref[...] = jnp.dot(a_ref[...], b_ref[...], preferred_element_type=F32).astype(o_ref.dtype)


def matmul(a, b, out_dtype, tm, tn):
    m, k = a.shape
    n = b.shape[1]
    tm, tn = min(tm, m), min(tn, n)
    assert m % tm == 0 and n % tn == 0, (m, n, tm, tn)
    return pl.pallas_call(
        _mm_kernel,
        grid=(m // tm, n // tn),
        in_specs=[pl.BlockSpec((tm, k), lambda i, j: (i, 0)),
                  pl.BlockSpec((k, tn), lambda i, j: (0, j))],
        out_specs=pl.BlockSpec((tm, tn), lambda i, j: (i, j)),
        out_shape=jax.ShapeDtypeStruct((m, n), out_dtype),
        compiler_params=_params("parallel", "arbitrary"),
        name="matmul",
    )(a, b)


def _gated_mm_kernel(a_ref, b1_ref, b2_ref, o_ref, *, gate_first):
    a = a_ref[...]
    y1 = jnp.dot(a, b1_ref[...], preferred_element_type=F32)
    y2 = jnp.dot(a, b2_ref[...], preferred_element_type=F32)
    if gate_first:
        o = (y1 * jax.nn.sigmoid(y1)) * y2
    else:
        o = y1 * jax.nn.sigmoid(y2)
    o_ref[...] = o.astype(o_ref.dtype)


def gated_matmul(a, w, out_dtype, tm, tn, gate_first):
    m, k = a.shape
    n = w.shape[1] // 2
    tm = min(tm, m)
    assert m % tm == 0 and n % tn == 0, (m, n, tm, tn)
    nb = n // tn
    return pl.pallas_call(
        functools.partial(_gated_mm_kernel, gate_first=gate_first),
        grid=(m // tm, nb),
        in_specs=[pl.BlockSpec((tm, k), lambda i, j: (i, 0)),
                  pl.BlockSpec((k, tn), lambda i, j: (0, j)),
                  pl.BlockSpec((k, tn), lambda i, j: (0, j + nb))],
        out_specs=pl.BlockSpec((tm, tn), lambda i, j: (i, j)),
        out_shape=jax.ShapeDtypeStruct((m, n), out_dtype),
        compiler_params=_params("parallel", "arbitrary"),
        name="gated_matmul",
    )(a, w, w)


def _rms(x, g):
    return x * lax.rsqrt(jnp.mean(x * x, axis=-1, keepdims=True) + NORM_EPS) * g


def _norm_kernel(x_ref, g_ref, o_ref):
    o_ref[...] = _rms(x_ref[...].astype(F32), g_ref[...]).astype(o_ref.dtype)


def rmsnorm_rows(x, g, out_dtype, tm=256):
    m, d = x.shape
    tm = min(tm, m)
    return pl.pallas_call(
        _norm_kernel,
        grid=(m // tm,),
        in_specs=[pl.BlockSpec((tm, d), lambda i: (i, 0)),
                  pl.BlockSpec((1, d), lambda i: (0, 0))],
        out_specs=pl.BlockSpec((tm, d), lambda i: (i, 0)),
        out_shape=jax.ShapeDtypeStruct((m, d), out_dtype),
        compiler_params=_params("parallel"),
        name="rmsnorm",
    )(x, g.reshape(1, d))


def _residual_norm_kernel(h_ref, y_ref, gpost_ref, gnext_ref, hout_ref, a_ref, *, coef):
    h = h_ref[...] + coef * _rms(y_ref[...].astype(F32), gpost_ref[...])
    hout_ref[...] = h
    a_ref[...] = _rms(h, gnext_ref[...]).astype(a_ref.dtype)


def residual_norm(h, y, g_post, g_next, coef, tm=256):
    m, d = h.shape
    tm = min(tm, m)
    row = pl.BlockSpec((tm, d), lambda i: (i, 0))
    vec = pl.BlockSpec((1, d), lambda i: (0, 0))
    return pl.pallas_call(
        functools.partial(_residual_norm_kernel, coef=coef),
        grid=(m // tm,),
        in_specs=[row, row, vec, vec],
        out_specs=[row, row],
        out_shape=[jax.ShapeDtypeStruct((m, d), F32), jax.ShapeDtypeStruct((m, d), BF16)],
        compiler_params=_params("parallel"),
        name="residual_norm",
    )(h, y, g_post.reshape(1, d), g_next.reshape(1, d))


XA_HEADS = 4
XA_HEAD_DIM = 128


def _xattn_kernel(a_ref, h_ref, wq_ref, k_ref, v_ref, wo_ref, gpost_ref, gnext_ref, hout_ref, anext_ref):
    q = jnp.dot(a_ref[...], wq_ref[...], preferred_element_type=F32)
    scale = XA_HEAD_DIM ** -0.5
    outs = []
    for hd in range(XA_HEADS):
        qh = q[:, hd * XA_HEAD_DIM:(hd + 1) * XA_HEAD_DIM].astype(BF16)
        s = lax.dot_general(qh, k_ref[hd], (((1,), (1,)), ((), ())), preferred_element_type=F32) * scale
        s = s - jnp.max(s, axis=-1, keepdims=True)
        p = jnp.exp(s)
        p = p / jnp.sum(p, axis=-1, keepdims=True)
        outs.append(jnp.dot(p.astype(BF16), v_ref[hd], preferred_element_type=F32))
    o = jnp.concatenate(outs, axis=-1).astype(BF16)
    y = jnp.dot(o, wo_ref[...], preferred_element_type=F32)
    h = h_ref[...] + _rms(y, gpost_ref[...])
    hout_ref[...] = h
    anext_ref[...] = _rms(h, gnext_ref[...]).astype(anext_ref.dtype)


def cross_attention_block(a, h, wq, k, v, wo, g_post, g_next, tm=256):
    t, d = h.shape
    tm = min(tm, t)
    hk = XA_HEADS * XA_HEAD_DIM
    mlen = k.shape[1]
    row = pl.BlockSpec((tm, d), lambda i: (i, 0))
    return pl.pallas_call(
        _xattn_kernel,
        grid=(t // tm,),
        in_specs=[row, row, _resident((d, hk)), _resident((XA_HEADS, mlen, XA_HEAD_DIM)),
                  _resident((XA_HEADS, mlen, XA_HEAD_DIM)), _resident((hk, d)),
                  _resident((1, d)), _resident((1, d))],
        out_specs=[row, row],
        out_shape=[jax.ShapeDtypeStruct((t, d), F32), jax.ShapeDtypeStruct((t, d), BF16)],
        compiler_params=_params("parallel"),
        name="cross_attention",
    )(a, h, wq, k, v, wo, g_post.reshape(1, d), g_next.reshape(1, d))


S5_GROUP = 16
S5_STATE = 64
S5_CHUNK = 128
S5_ROWS = 8
S5_BLOCK_GROUPS = 8
S5_BLOCK_IN = S5_BLOCK_GROUPS * S5_GROUP
S5_BLOCK_STATE = S5_BLOCK_GROUPS * S5_STATE


def _s5_kernel(u_ref, wb_ref, cs_ref, are_ref, aim_ref, d_ref, z_ref, bre, bim, sre, sim, *, chunk, n_blocks, lanes):
    blocks_per_row = lanes // S5_BLOCK_STATE
    tiles_per_block = S5_BLOCK_STATE // LANES
    tiles_per_pass = 4

    @pl.when(pl.program_id(0) == 0)
    def _():
        sre[...] = jnp.zeros_like(sre)
        sim[...] = jnp.zeros_like(sim)

    for b in range(n_blocks):
        r, tile0 = b // blocks_per_row, (b % blocks_per_row) * tiles_per_block
        bu = jnp.dot(u_ref[:, b * S5_BLOCK_IN:(b + 1) * S5_BLOCK_IN], wb_ref[b], preferred_element_type=F32)
        for j in range(tiles_per_block):
            bre[tile0 + j, pl.ds(r, chunk, stride=S5_ROWS), :] = bu[:, j * LANES:(j + 1) * LANES]
            bim[tile0 + j, pl.ds(r, chunk, stride=S5_ROWS), :] = bu[:, S5_BLOCK_STATE + j * LANES:S5_BLOCK_STATE + (j + 1) * LANES]

    for c0 in range(0, lanes // LANES, tiles_per_pass):
        sl = slice(c0, c0 + tiles_per_pass)
        a_re, a_im = are_ref[sl], aim_ref[sl]

        def step(t, carry, sl=sl, a_re=a_re, a_im=a_im):
            s_re, s_im = carry
            row = pl.multiple_of(t * S5_ROWS, S5_ROWS)
            n_re = a_re * s_re - a_im * s_im + bre[sl, pl.ds(row, S5_ROWS), :]
            n_im = a_re * s_im + a_im * s_re + bim[sl, pl.ds(row, S5_ROWS), :]
            bre[sl, pl.ds(row, S5_ROWS), :] = n_re
            bim[sl, pl.ds(row, S5_ROWS), :] = n_im
            return n_re, n_im

        s_re, s_im = lax.fori_loop(0, chunk, step, (sre[sl], sim[sl]))
        sre[sl] = s_re
        sim[sl] = s_im

    for b in range(n_blocks):
        r, tile0 = b // blocks_per_row, (b % blocks_per_row) * tiles_per_block
        parts = [bre[tile0 + j, pl.ds(r, chunk, stride=S5_ROWS), :] for j in range(tiles_per_block)]
        parts += [bim[tile0 + j, pl.ds(r, chunk, stride=S5_ROWS), :] for j in range(tiles_per_block)]
        hist = jnp.concatenate(parts, axis=-1).astype(BF16)
        cols = slice(b * S5_BLOCK_IN, (b + 1) * S5_BLOCK_IN)
        y = jnp.dot(hist, cs_ref[b], preferred_element_type=F32) + d_ref[:, cols] * u_ref[:, cols].astype(F32)
        z_ref[:, cols] = jax.nn.gelu(y).astype(z_ref.dtype)


def _block_diag(w, n_blocks):
    g, a, b = w.shape
    eye = jnp.eye(S5_BLOCK_GROUPS, dtype=w.dtype)
    w = w.reshape(n_blocks, S5_BLOCK_GROUPS, a, b)
    return jnp.einsum('ngab,gh->ngahb', w, eye).reshape(n_blocks, S5_BLOCK_GROUPS * a, S5_BLOCK_GROUPS * b)


def _state_tiles(x, lanes):
    return jnp.transpose(x.reshape(S5_ROWS, lanes // LANES, LANES), (1, 0, 2))


def s5_scan(u, lam_re, lam_im, log_step, b_re, b_im, c_re, c_im, d_skip):
    t, d = u.shape
    groups = d // S5_GROUP
    n_blocks = groups // S5_BLOCK_GROUPS
    lanes = groups * S5_STATE // S5_ROWS
    chunk = min(S5_CHUNK, t)
    dt = jnp.exp(log_step.astype(F32))[:, None]
    lr = jnp.minimum(lam_re.astype(F32), -1e-4)
    li = lam_im.astype(F32)
    mag = jnp.exp(lr * dt)
    a_re = mag * jnp.cos(li * dt)
    a_im = mag * jnp.sin(li * dt)
    den = lr * lr + li * li
    coef_re = ((a_re - 1.0) * lr + a_im * li) / den
    coef_im = (a_im * lr - (a_re - 1.0) * li) / den
    br, bi_ = b_re.astype(F32), b_im.astype(F32)
    bb_re = coef_re[..., None] * br - coef_im[..., None] * bi_
    bb_im = coef_re[..., None] * bi_ + coef_im[..., None] * br
    wb = jnp.concatenate([_block_diag(jnp.swapaxes(bb_re, 1, 2), n_blocks),
                          _block_diag(jnp.swapaxes(bb_im, 1, 2), n_blocks)], axis=-1).astype(BF16)
    cs = jnp.concatenate([_block_diag(jnp.swapaxes(c_re.astype(F32), 1, 2), n_blocks),
                          _block_diag(-jnp.swapaxes(c_im.astype(F32), 1, 2), n_blocks)], axis=1).astype(BF16)
    return pl.pallas_call(
        functools.partial(_s5_kernel, chunk=chunk, n_blocks=n_blocks, lanes=lanes),
        grid=(t // chunk,),
        in_specs=[pl.BlockSpec((chunk, d), lambda i: (i, 0)),
                  _resident(wb.shape), _resident(cs.shape),
                  _resident((lanes // LANES, S5_ROWS, LANES)), _resident((lanes // LANES, S5_ROWS, LANES)),
                  _resident((1, d))],
        out_specs=pl.BlockSpec((chunk, d), lambda i: (i, 0)),
        out_shape=jax.ShapeDtypeStruct((t, d), BF16),
        scratch_shapes=[pltpu.VMEM((lanes // LANES, chunk * S5_ROWS, LANES), F32),
                        pltpu.VMEM((lanes // LANES, chunk * S5_ROWS, LANES), F32),
                        pltpu.VMEM((lanes // LANES, S5_ROWS, LANES), F32),
                        pltpu.VMEM((lanes // LANES, S5_ROWS, LANES), F32)],
        compiler_params=_params("arbitrary"),
        name="s5_scan",
    )(u, wb, cs, _state_tiles(a_re, lanes), _state_tiles(a_im, lanes), d_skip.astype(F32).reshape(1, d))


def ffn_block(a, h, w_in, w_out, g_post, g_next):
    act = gated_matmul(a, w_in.astype(BF16), BF16, tm=1024, tn=512, gate_first=True)
    y = matmul(act, w_out.astype(BF16), F32, tm=1024, tn=512)
    return residual_norm(h, y, g_post, g_next, 0.5)


def cross_attn_block(a, h, mem, mem_norm, wq, wkv, wo, g_post, g_next):
    mlen = mem.shape[0]
    memn = rmsnorm_rows(mem, mem_norm, BF16)
    kv = matmul(memn, wkv.astype(BF16), BF16, tm=256, tn=512).reshape(mlen, 2, XA_HEADS, XA_HEAD_DIM)
    k = jnp.transpose(kv[:, 0], (1, 0, 2))
    v = jnp.transpose(kv[:, 1], (1, 0, 2))
    return cross_attention_block(a, h, wq.astype(BF16), k, v, wo.astype(BF16), g_post, g_next)


def s5_mixer(a, lam_re, lam_im, log_step, b_re, b_im, c_re, c_im, d_skip, w_glu):
    z = s5_scan(a, lam_re, lam_im, log_step, b_re, b_im, c_re, c_im, d_skip)
    return gated_matmul(z, w_glu.astype(BF16), F32, tm=1024, tn=512, gate_first=False)


def rope_tables(pos, d):
    inv = ROPE_THETA ** (-jnp.arange(0, d, 2, dtype=F32) / d)
    ang = pos.astype(F32)[:, None] * inv
    cos, sin = jnp.cos(ang), jnp.sin(ang)
    return jnp.concatenate([cos, cos], axis=-1), jnp.concatenate([-sin, sin], axis=-1)


def _rope(x, cosf, sinf):
    return x * cosf + pltpu.roll(x, x.shape[-1] // 2, axis=1) * sinf


def _dot_nt(a, b):
    return lax.dot_general(a, b, (((1,), (1,)), ((), ())), preferred_element_type=F32)


def _online_softmax_step(s, bias, v_blk, m_ref, l_ref, acc_ref, heads, qb):
    kb = s.shape[-1]
    s = (s.reshape(heads, qb, kb) + bias[None]).reshape(heads * qb, kb)
    m_prev = m_ref[...]
    m_new = jnp.maximum(m_prev, jnp.max(s, axis=-1, keepdims=True))
    alpha = jnp.exp(m_prev - m_new)
    p = jnp.exp(s - m_new)
    l_ref[...] = alpha * l_ref[...] + jnp.sum(p, axis=-1, keepdims=True)
    acc_ref[...] = alpha * acc_ref[...] + jnp.dot(p.astype(BF16), v_blk, preferred_element_type=F32)
    m_ref[...] = m_new


def _reset_softmax_state(m_ref, l_ref, acc_ref):
    m_ref[...] = jnp.full_like(m_ref, MASK_NEG)
    l_ref[...] = jnp.zeros_like(l_ref)
    acc_ref[...] = jnp.zeros_like(acc_ref)


NSA_HEADS = 32
NSA_GROUPS = 4
NSA_HPG = NSA_HEADS // NSA_GROUPS
NSA_DH = 128
NSA_KV_WIDTH = NSA_GROUPS * NSA_DH
NSA_CMP_STRIDE = 16
NSA_CMP_LEN = 2 * NSA_CMP_STRIDE
NSA_SEL_BLOCK = 64
NSA_CMP_PER_SEL = NSA_SEL_BLOCK // NSA_CMP_STRIDE
NSA_N_SEL = 16
NSA_WINDOW = 512
NSA_FORCE = 1e4
NSA_GATE_COLS = 3 * NSA_HPG
NSA_KEY_CHUNK = 512


def _nsa_prep_kernel(kc_ref, vc_ref, ks_ref, vs_ref, kw_ref, vw_ref, cos_ref, sin_ref, okvc, oks, ovs, okw, ovw):
    cosf, sinf = cos_ref[...], sin_ref[...]
    for g in range(NSA_GROUPS):
        sl = slice(g * NSA_DH, (g + 1) * NSA_DH)
        okvc[0, g] = kc_ref[:, sl]
        okvc[1, g] = vc_ref[:, sl]
        ovs[g] = vs_ref[:, sl]
        ovw[g] = vw_ref[:, sl]
        oks[g] = _rope(ks_ref[:, sl].astype(F32), cosf, sinf).astype(BF16)
        okw[g] = _rope(kw_ref[:, sl].astype(F32), cosf, sinf).astype(BF16)


def nsa_prep(qkv, cosf, sinf, tb=512):
    t = qkv.shape[0]
    tb = min(tb, t)
    first = NSA_HEADS * NSA_DH // NSA_KV_WIDTH
    piece = lambda p: pl.BlockSpec((tb, NSA_KV_WIDTH), lambda i, p=p: (i, first + p))
    tab = pl.BlockSpec((tb, NSA_DH), lambda i: (i, 0))
    grouped = pl.BlockSpec((NSA_GROUPS, tb, NSA_DH), lambda i: (0, i, 0))
    gshape = jax.ShapeDtypeStruct((NSA_GROUPS, t, NSA_DH), BF16)
    return pl.pallas_call(
        _nsa_prep_kernel,
        grid=(t // tb,),
        in_specs=[piece(p) for p in range(6)] + [tab, tab],
        out_specs=[pl.BlockSpec((2, NSA_GROUPS, tb, NSA_DH), lambda i: (0, 0, i, 0)), grouped, grouped, grouped, grouped],
        out_shape=[jax.ShapeDtypeStruct((2, NSA_GROUPS, t, NSA_DH), BF16), gshape, gshape, gshape, gshape],
        compiler_params=_params("parallel"),
        name="nsa_prep",
    )(qkv, qkv, qkv, qkv, qkv, qkv, cosf, sinf)


def _nsa_cmp_kernel(x_ref, pos_ref, w1_ref, w2_ref, o_ref, *, nc):
    half = NSA_CMP_STRIDE * NSA_DH
    x = x_ref[0, 0].astype(F32)
    pos_a, pos_b = pos_ref[0, 0:1, :], pos_ref[0, 1:2, :]
    w1a, w1b = w1_ref[0, :half, :], w1_ref[0, half:, :]
    y1 = jnp.dot((x + pos_a).astype(BF16), w1a, preferred_element_type=F32)
    y2 = jnp.dot((x + pos_b).astype(BF16), w1b, preferred_element_type=F32)
    y_pad = jnp.dot(jnp.broadcast_to(pos_b, (8, half)).astype(BF16), w1b, preferred_element_type=F32)[0:1]
    rows = lax.broadcasted_iota(jnp.int32, (nc, 1), 0)
    y2_next = jnp.where(rows == nc - 1, y_pad, pltpu.roll(y2, nc - 1, axis=0))
    pre = y1 + y2_next
    hid = (pre * jax.nn.sigmoid(pre)).astype(BF16)
    o_ref[0, 0] = jnp.dot(hid, w2_ref[0], preferred_element_type=F32).astype(o_ref.dtype)


def nsa_compress(kvc, cmp_pos, cmp_w1, cmp_w2):
    _, g, t, dh = kvc.shape
    nc = t // NSA_CMP_STRIDE
    row = NSA_CMP_STRIDE * dh
    x = kvc.reshape(2, g, nc, row)
    return pl.pallas_call(
        functools.partial(_nsa_cmp_kernel, nc=nc),
        grid=(2, g),
        in_specs=[pl.BlockSpec((1, 1, nc, row), lambda kv, gi: (kv, gi, 0, 0)),
                  pl.BlockSpec((1, 2, row), lambda kv, gi: (kv, 0, 0)),
                  pl.BlockSpec((1, 2 * row, dh), lambda kv, gi: (kv, 0, 0)),
                  pl.BlockSpec((1, dh, dh), lambda kv, gi: (kv, 0, 0))],
        out_specs=pl.BlockSpec((1, 1, nc, dh), lambda kv, gi: (kv, gi, 0, 0)),
        out_shape=jax.ShapeDtypeStruct((2, g, nc, dh), BF16),
        compiler_params=_params("parallel", "parallel"),
        name="nsa_compress",
    )(x, cmp_pos.astype(F32).reshape(2, 2, row), cmp_w1.astype(BF16), cmp_w2.astype(BF16))


def _nsa_attn_kernel(q_ref, cos_ref, sin_ref, gate_ref, kc_ref, vc_ref, ks_ref, vs_ref, kw_ref, vw_ref, o_ref,
                     m_ref, l_ref, acc_ref, *, nc, ns, n_sel, kchunk):
    bi = pl.program_id(1)
    qb, hpg, dh = Q_BLOCK, NSA_HPG, NSA_DH
    start = bi * qb
    tq = start + lax.broadcasted_iota(jnp.int32, (qb, 1), 0)
    scale = dh ** -0.5
    cosf, sinf = cos_ref[...], sin_ref[...]
    q_un, q_rot = [], []
    for h in range(hpg):
        qh = q_ref[:, h * dh:(h + 1) * dh].astype(F32)
        q_un.append((qh * scale).astype(BF16))
        q_rot.append((_rope(qh, cosf, sinf) * scale).astype(BF16))
    q_un = jnp.concatenate(q_un, axis=0)
    q_rot = jnp.concatenate(q_rot, axis=0)

    s3 = _dot_nt(q_un, kc_ref[0]).reshape(hpg, qb, nc)
    lane = lax.broadcasted_iota(jnp.int32, (1, nc), 1)
    cmp_idx = (lane % ns) * NSA_CMP_PER_SEL + lane // ns
    cmask = (cmp_idx * NSA_CMP_STRIDE + (NSA_CMP_LEN - 1) <= tq)[None]
    s3 = jnp.where(cmask, s3, MASK_NEG)
    p = jnp.where(cmask, jnp.exp(s3 - jnp.max(s3, axis=-1, keepdims=True)), 0.0)
    p = p * (1.0 / jnp.maximum(jnp.sum(p, axis=-1, keepdims=True), 1e-30))
    o_c = jnp.dot(p.reshape(hpg * qb, nc).astype(BF16), vc_ref[0], preferred_element_type=F32)
    imp_c = jnp.sum(p, axis=0)
    imp = imp_c[:, 0:ns]
    for r in range(1, NSA_CMP_PER_SEL):
        imp = imp + imp_c[:, r * ns:(r + 1) * ns]

    ids = lax.broadcasted_iota(jnp.int32, (1, ns), 1)
    ids_f = ids.astype(F32)
    cur = tq // NSA_SEL_BLOCK
    valid = ids * NSA_SEL_BLOCK <= tq
    forced = (ids == 0) | (ids == cur) | (ids == cur - 1)
    val = jnp.where(forced, NSA_FORCE, jnp.where(valid, imp, -1.0))
    sel = jnp.zeros((qb, ns), F32)
    for _ in range(n_sel):
        mx = jnp.max(val, axis=-1, keepdims=True)
        first = jnp.min(jnp.where(val == mx, ids_f, float(ns)), axis=-1, keepdims=True)
        pick = ids_f == first
        sel = jnp.where(pick, 1.0, sel)
        val = jnp.where(pick, -3e38, val)
    sel_b = sel.astype(BF16)

    _reset_softmax_state(m_ref, l_ref, acc_ref)
    blk_row = lax.broadcasted_iota(jnp.int32, (ns, kchunk), 0)
    blk_of_key = lax.broadcasted_iota(jnp.int32, (ns, kchunk), 1) // NSA_SEL_BLOCK
    key_lane = lax.broadcasted_iota(jnp.int32, (1, kchunk), 1)

    def sel_body(ci, carry):
        k0 = pl.multiple_of(ci * kchunk, kchunk)
        s = _dot_nt(q_rot, ks_ref[0, pl.ds(k0, kchunk), :])
        expand = jnp.where(blk_row == blk_of_key + k0 // NSA_SEL_BLOCK, 1.0, 0.0).astype(BF16)
        selx = jnp.dot(sel_b, expand, preferred_element_type=F32)
        bias = jnp.where((selx > 0.5) & (key_lane + k0 <= tq), 0.0, MASK_NEG)
        _online_softmax_step(s, bias, vs_ref[0, pl.ds(k0, kchunk), :], m_ref, l_ref, acc_ref, hpg, qb)
        return carry

    lax.fori_loop(0, (start + qb + kchunk - 1) // kchunk, sel_body, 0)
    o_s = acc_ref[...] / l_ref[...]

    _reset_softmax_state(m_ref, l_ref, acc_ref)
    n_wblk = NSA_WINDOW // qb + 1
    wlane = lax.broadcasted_iota(jnp.int32, (1, qb), 1)

    def win_body(j, carry):
        k0 = pl.multiple_of((bi - (n_wblk - 1) + j) * qb, qb)
        s = _dot_nt(q_rot, kw_ref[0, pl.ds(k0, qb), :])
        diff = tq - (wlane + k0)
        bias = jnp.where((diff >= 0) & (diff < NSA_WINDOW), 0.0, MASK_NEG)
        _online_softmax_step(s, bias, vw_ref[0, pl.ds(k0, qb), :], m_ref, l_ref, acc_ref, hpg, qb)
        return carry

    lax.fori_loop(jnp.maximum(0, n_wblk - 1 - bi), n_wblk, win_body, 0)
    o_w = acc_ref[...] / l_ref[...]

    gs = jax.nn.sigmoid(gate_ref[0])
    for h in range(hpg):
        rows = slice(h * qb, (h + 1) * qb)
        o = (gs[:, 3 * h:3 * h + 1] * o_c[rows] + gs[:, 3 * h + 1:3 * h + 2] * o_s[rows]
             + gs[:, 3 * h + 2:3 * h + 3] * o_w[rows])
        o_ref[:, h * dh:(h + 1) * dh] = o.astype(o_ref.dtype)


def nsa_attention(qkv, cosf, sinf, gate, k_cmp, v_cmp, ks, vs, kw, vw):
    g, t, dh = ks.shape
    nc, ns = t // NSA_CMP_STRIDE, t // NSA_SEL_BLOCK
    nb = t // Q_BLOCK
    rows = NSA_HPG * Q_BLOCK
    kchunk = min(NSA_KEY_CHUNK, t)
    whole = lambda n: pl.BlockSpec((1, n, dh), lambda gi, i: (gi, 0, 0))
    kernel = functools.partial(_nsa_attn_kernel, nc=nc, ns=ns, n_sel=min(NSA_N_SEL, ns), kchunk=kchunk)
    return pl.pallas_call(
        kernel,
        grid=(g, nb),
        in_specs=[pl.BlockSpec((Q_BLOCK, NSA_HPG * dh), lambda gi, i: (i, gi)),
                  pl.BlockSpec((Q_BLOCK, dh), lambda gi, i: (i, 0)),
                  pl.BlockSpec((Q_BLOCK, dh), lambda gi, i: (i, 0)),
                  pl.BlockSpec((1, Q_BLOCK, LANES), lambda gi, i: (gi, i, 0)),
                  whole(nc), whole(nc), whole(t), whole(t), whole(t), whole(t)],
        out_specs=pl.BlockSpec((Q_BLOCK, NSA_HPG * dh), lambda gi, i: (i, gi)),
        out_shape=jax.ShapeDtypeStruct((t, NSA_HEADS * dh), BF16),
        scratch_shapes=[pltpu.VMEM((rows, 1), F32), pltpu.VMEM((rows, 1), F32), pltpu.VMEM((rows, dh), F32)],
        compiler_params=_params("parallel", "arbitrary"),
        name="nsa_attention",
    )(qkv, cosf, sinf, gate, k_cmp, v_cmp, ks, vs, kw, vw)


def _pad_cols(w, multiple):
    n = w.shape[1]
    return jnp.pad(w, ((0, 0), (0, -n % multiple)))


def nsa_mixer(a, pos, w_in, cmp_pos, cmp_w1, cmp_w2, w_out):
    t = a.shape[0]
    g, dh = NSA_GROUPS, NSA_DH
    qkv = matmul(a, _pad_cols(w_in.astype(BF16), 256), BF16, tm=1024, tn=256)
    cosf, sinf = rope_tables(pos, dh)
    kvc, ks, vs, kw, vw = nsa_prep(qkv, cosf, sinf)
    cmp = nsa_compress(kvc, cmp_pos, cmp_w1, cmp_w2)
    nc, ns = t // NSA_CMP_STRIDE, t // NSA_SEL_BLOCK
    cmp = cmp.reshape(2, g, ns, NSA_CMP_PER_SEL, dh).transpose(0, 1, 3, 2, 4).reshape(2, g, nc, dh)
    gate0 = NSA_HEADS * dh + 6 * NSA_KV_WIDTH
    gate = qkv[:, gate0:gate0 + NSA_HEADS * 3].astype(F32).reshape(t, g, NSA_GATE_COLS).transpose(1, 0, 2)
    gate = jnp.pad(gate, ((0, 0), (0, 0), (0, LANES - NSA_GATE_COLS)))
    o = nsa_attention(qkv, cosf, sinf, gate, cmp[0], cmp[1], ks, vs, kw, vw)
    return matmul(o, w_out.astype(BF16), F32, tm=1024, tn=512)


DSA_HEADS = 32
DSA_NOPE = 128
DSA_ROPE = 64
DSA_V = 128
DSA_LATENT = 512
DSA_IDX_HEADS = 32
DSA_IDX_DIM = 128
DSA_TOPK_MAX = 256
DSA_KEY_CHUNK = 512
DSA_Q_TILE = 512
INT32_MIN = -2 ** 31
DSA_QN0 = 0
DSA_QIDX0 = DSA_QN0 + DSA_HEADS * DSA_NOPE
DSA_QPE0 = DSA_QIDX0 + DSA_IDX_HEADS * DSA_IDX_DIM
DSA_CKV0 = DSA_QPE0 + DSA_HEADS * DSA_ROPE
DSA_KPE0 = DSA_CKV0 + DSA_LATENT
DSA_KIDX0 = DSA_KPE0 + LANES
DSA_WIDX0 = DSA_KIDX0 + DSA_IDX_DIM


def _dsa_w_in_layout(w_in):
    h = DSA_HEADS
    sizes = [h * DSA_NOPE, h * DSA_ROPE, DSA_LATENT, DSA_ROPE, DSA_IDX_HEADS * DSA_IDX_DIM, DSA_IDX_DIM, DSA_IDX_HEADS]
    offs = [0]
    for s in sizes:
        offs.append(offs[-1] + s)
    q_nope, q_pe, c_kv, k_pe, q_idx, k_idx, w_idx = [w_in[:, offs[i]:offs[i + 1]] for i in range(7)]
    scale = (DSA_NOPE + DSA_ROPE) ** -0.5
    idx_scale = (DSA_IDX_HEADS * DSA_IDX_DIM) ** -0.5
    pad = lambda x: jnp.pad(x, ((0, 0), (0, LANES - x.shape[1])))
    w = jnp.concatenate([q_nope * scale, q_idx, q_pe * scale, c_kv, pad(k_pe), k_idx, pad(w_idx * idx_scale)], axis=1)
    return _pad_cols(w.astype(BF16), 256)


def _rope64_pairs(x, cosf, sinf):
    lane = lax.broadcasted_iota(jnp.int32, x.shape, 1)
    swapped = jnp.where(lane % DSA_ROPE < DSA_ROPE // 2, pltpu.roll(x, LANES - DSA_ROPE // 2, axis=1),
                        pltpu.roll(x, DSA_ROPE // 2, axis=1))
    return x * cosf + swapped * sinf


def _dsa_prep_kernel(qidx_ref, qpe_ref, ckv_ref, kpe_ref, kidx_ref, cos_ref, sin_ref, cos64_ref, sin64_ref, g_ref,
                     oqidx, oqpe, ockv, okpe, okidx):
    cosf, sinf = cos_ref[...], sin_ref[...]
    cos64, sin64 = cos64_ref[...], sin64_ref[...]
    low = lax.broadcasted_iota(jnp.int32, cosf.shape, 1) < DSA_ROPE
    for h in range(DSA_IDX_HEADS):
        sl = slice(h * DSA_IDX_DIM, (h + 1) * DSA_IDX_DIM)
        oqidx[:, sl] = _rope(qidx_ref[:, sl].astype(F32), cosf, sinf).astype(BF16)
    okidx[...] = _rope(kidx_ref[...].astype(F32), cosf, sinf).astype(BF16)
    for j in range(DSA_HEADS // 2):
        r = _rope64_pairs(qpe_ref[:, j * LANES:(j + 1) * LANES].astype(F32), cos64, sin64)
        oqpe[:, (2 * j) * LANES:(2 * j + 1) * LANES] = jnp.where(low, r, 0.0).astype(BF16)
        oqpe[:, (2 * j + 1) * LANES:(2 * j + 2) * LANES] = jnp.where(low, pltpu.roll(r, DSA_ROPE, axis=1), 0.0).astype(BF16)
    okpe[...] = jnp.where(low, _rope64_pairs(kpe_ref[...].astype(F32), cos64, sin64), 0.0).astype(BF16)
    ockv[...] = _rms(ckv_ref[...].astype(F32), g_ref[...]).astype(BF16)


def dsa_prep(qkv, cosf, sinf, cos64, sin64, kv_norm, tb=256):
    t = qkv.shape[0]
    tb = min(tb, t)
    col = lambda width, off: pl.BlockSpec((tb, width), lambda i: (i, off // width))
    tab = pl.BlockSpec((tb, LANES), lambda i: (i, 0))
    wide = DSA_IDX_HEADS * DSA_IDX_DIM
    out = lambda width: pl.BlockSpec((tb, width), lambda i: (i, 0))
    return pl.pallas_call(
        _dsa_prep_kernel,
        grid=(t // tb,),
        in_specs=[col(wide, DSA_QIDX0), col(DSA_HEADS * DSA_ROPE, DSA_QPE0), col(DSA_LATENT, DSA_CKV0),
                  col(LANES, DSA_KPE0), col(LANES, DSA_KIDX0), tab, tab, tab, tab,
                  pl.BlockSpec((1, DSA_LATENT), lambda i: (0, 0))],
        out_specs=[out(wide), out(DSA_HEADS * LANES), out(DSA_LATENT), out(LANES), out(LANES)],
        out_shape=[jax.ShapeDtypeStruct((t, wide), BF16), jax.ShapeDtypeStruct((t, DSA_HEADS * LANES), BF16),
                   jax.ShapeDtypeStruct((t, DSA_LATENT), BF16), jax.ShapeDtypeStruct((t, LANES), BF16),
                   jax.ShapeDtypeStruct((t, LANES), BF16)],
        compiler_params=_params("parallel"),
        name="dsa_prep",
    )(qkv, qkv, qkv, qkv, qkv, cosf, sinf, cos64, sin64, kv_norm.astype(F32).reshape(1, DSA_LATENT))


def _dsa_index_kernel(qi_ref, wi_ref, kidx_ref, mask_ref, keys_ref, *, topk, kchunk, idx_bits, heads_per_dot):
    bi = pl.program_id(0)
    qb = Q_BLOCK
    start = bi * qb
    tq = start + lax.broadcasted_iota(jnp.int32, (qb, 1), 0)
    n_chunks = (start + qb + kchunk - 1) // kchunk
    w = wi_ref[...].astype(F32)
    lane = lax.broadcasted_iota(jnp.int32, (1, kchunk), 1)

    def score_body(ci, carry):
        k0 = pl.multiple_of(ci * kchunk, kchunk)
        k_blk = kidx_ref[pl.ds(k0, kchunk), :]
        acc = jnp.zeros((qb, kchunk), F32)
        for h0 in range(0, DSA_IDX_HEADS, heads_per_dot):
            qs = jnp.concatenate([qi_ref[:, h * DSA_IDX_DIM:(h + 1) * DSA_IDX_DIM]
                                  for h in range(h0, h0 + heads_per_dot)], axis=0)
            logits = _dot_nt(qs, k_blk)
            for j in range(heads_per_dot):
                acc = acc + w[:, h0 + j:h0 + j + 1] * jnp.maximum(logits[j * qb:(j + 1) * qb], 0.0)
        bits = pltpu.bitcast(acc, jnp.int32)
        key = jnp.where(bits < 0, bits ^ 0x7FFFFFFF, bits)
        keys_ref[:, pl.ds(k0, kchunk)] = jnp.where(lane + k0 <= tq, key, INT32_MIN)
        return carry

    lax.fori_loop(0, n_chunks, score_body, 0)

    def count(pred):
        def body(ci, cnt):
            k0 = pl.multiple_of(ci * kchunk, kchunk)
            return cnt + jnp.where(pred(keys_ref[:, pl.ds(k0, kchunk)], lane + k0), 1.0, 0.0)
        cnt = lax.fori_loop(0, n_chunks, body, jnp.zeros((qb, kchunk), F32))
        return jnp.sum(cnt, axis=-1, keepdims=True)

    def value_bit(i, thr):
        cand = thr + lax.shift_left(jnp.int32(1), 31 - i)
        return jnp.where(count(lambda key, idx: key >= cand) >= topk, cand, thr)

    thr = lax.fori_loop(0, 32, value_bit, jnp.full((qb, 1), INT32_MIN, jnp.int32))
    n_gt = count(lambda key, idx: key > thr)
    n_eq = count(lambda key, idx: key == thr)
    need = topk - n_gt

    def index_search():
        def index_bit(i, lim):
            cand = lim + lax.shift_left(jnp.int32(1), idx_bits - 1 - i)
            return jnp.where(count(lambda key, idx: (key == thr) & (idx < cand)) < need, cand, lim)
        return lax.fori_loop(0, idx_bits, index_bit, jnp.zeros((qb, 1), jnp.int32))

    all_ties = jnp.full((qb, 1), (1 << idx_bits) - 1, jnp.int32)
    lim = lax.cond(jnp.max(n_eq - need) > 0.0, index_search, lambda: all_ties)

    def pack_body(ci, m):
        k0 = pl.multiple_of(ci * kchunk, kchunk)
        key = keys_ref[:, pl.ds(k0, kchunk)]
        idx = lane + k0
        chosen = ((key > thr) | ((key == thr) & (idx <= lim))) & (idx <= tq)
        return m | jnp.where(chosen, lax.shift_left(jnp.int32(1), ci), 0)

    mask_ref[...] = lax.fori_loop(0, n_chunks, pack_body, jnp.zeros((qb, kchunk), jnp.int32))


def dsa_index(qidx, qkv, kidx, topk):
    t = qidx.shape[0]
    kchunk = min(DSA_KEY_CHUNK, t)
    assert t // kchunk <= 32, "selection bits are packed into one int32 word per key lane"
    idx_bits = max(1, (t - 1).bit_length())
    kernel = functools.partial(_dsa_index_kernel, topk=float(topk), kchunk=kchunk, idx_bits=idx_bits, heads_per_dot=8)
    return pl.pallas_call(
        kernel,
        grid=(t // Q_BLOCK,),
        in_specs=[pl.BlockSpec((Q_BLOCK, DSA_IDX_HEADS * DSA_IDX_DIM), lambda i: (i, 0)),
                  pl.BlockSpec((Q_BLOCK, LANES), lambda i: (i, DSA_WIDX0 // LANES)),
                  _resident((t, DSA_IDX_DIM))],
        out_specs=pl.BlockSpec((Q_BLOCK, kchunk), lambda i: (i, 0)),
        out_shape=jax.ShapeDtypeStruct((t, kchunk), jnp.int32),
        scratch_shapes=[pltpu.VMEM((Q_BLOCK, t), jnp.int32)],
        compiler_params=_params("parallel"),
        name="dsa_index",
    )(qidx, qkv, kidx)


def _dsa_attn_kernel(qn_ref, qpe_ref, kn_ref, kpe_ref, v_ref, mask_ref, o_ref, m_ref, l_ref, acc_ref, *, kchunk):
    j = pl.program_id(1)
    tq = qn_ref.shape[0]
    q = jnp.concatenate([qn_ref[...], qpe_ref[...]], axis=1)
    mbits = mask_ref[...]
    _reset_softmax_state(m_ref, l_ref, acc_ref)

    def body(ci, carry):
        k0 = pl.multiple_of(ci * kchunk, kchunk)
        k_cat = jnp.concatenate([kn_ref[pl.ds(k0, kchunk), :], kpe_ref[pl.ds(k0, kchunk), :]], axis=1)
        s = _dot_nt(q, k_cat)
        bit = lax.shift_right_logical(mbits, ci) & 1
        bias = jnp.where(bit == 1, 0.0, MASK_NEG)
        _online_softmax_step(s, bias, v_ref[pl.ds(k0, kchunk), :], m_ref, l_ref, acc_ref, 1, tq)
        return carry

    lax.fori_loop(0, ((j + 1) * tq + kchunk - 1) // kchunk, body, 0)
    o_ref[...] = (acc_ref[...] / l_ref[...]).astype(o_ref.dtype)


def dsa_attention(qkv, qpe, kv, kpe, mask):
    t = qkv.shape[0]
    kchunk = mask.shape[1]
    tq = min(DSA_Q_TILE, t)
    h = DSA_HEADS
    return pl.pallas_call(
        functools.partial(_dsa_attn_kernel, kchunk=kchunk),
        grid=(h, t // tq),
        in_specs=[pl.BlockSpec((tq, DSA_NOPE), lambda hi, j: (j, hi)),
                  pl.BlockSpec((tq, LANES), lambda hi, j: (j, hi)),
                  pl.BlockSpec((t, DSA_NOPE), lambda hi, j: (0, hi)),
                  _resident((t, LANES)),
                  pl.BlockSpec((t, DSA_V), lambda hi, j: (0, h + hi)),
                  pl.BlockSpec((tq, kchunk), lambda hi, j: (j, 0))],
        out_specs=pl.BlockSpec((tq, DSA_V), lambda hi, j: (j, hi)),
        out_shape=jax.ShapeDtypeStruct((t, h * DSA_V), BF16),
        scratch_shapes=[pltpu.VMEM((tq, 1), F32), pltpu.VMEM((tq, 1), F32), pltpu.VMEM((tq, DSA_V), F32)],
        compiler_params=_params("parallel", "arbitrary"),
        name="dsa_attention",
    )(qkv, qpe, kv, kpe, kv, mask)


def dsa_mixer(a, pos, w_in, kv_norm, w_uk, w_uv, w_out):
    t = a.shape[0]
    qkv = matmul(a, _dsa_w_in_layout(w_in), BF16, tm=1024, tn=256)
    cosf, sinf = rope_tables(pos, DSA_IDX_DIM)
    cos64, sin64 = rope_tables(pos, DSA_ROPE)
    cos64, sin64 = jnp.tile(cos64, (1, 2)), jnp.tile(sin64, (1, 2))
    qidx, qpe, ckv, kpe, kidx = dsa_prep(qkv, cosf, sinf, cos64, sin64, kv_norm)
    w_kv = jnp.concatenate([jnp.transpose(w_uk, (1, 0, 2)).reshape(DSA_LATENT, DSA_HEADS * DSA_NOPE),
                            jnp.transpose(w_uv, (1, 0, 2)).reshape(DSA_LATENT, DSA_HEADS * DSA_V)], axis=1)
    kv = matmul(ckv, w_kv.astype(BF16), BF16, tm=1024, tn=512)
    mask = dsa_index(qidx, qkv, kidx, min(DSA_TOPK_MAX, t // 4))
    o = dsa_attention(qkv, qpe, kv, kpe, mask)
    return matmul(o, w_out.astype(BF16), F32, tm=1024, tn=512)


N_MIXERS = 3


def _trunk(x, mem, pos, norm_gains, ffn_w_in, ffn_w_out, xa_mem_norm, xa_wq, xa_wkv, xa_wo,
           nsa_params, dsa_params, s5_params):
    depth = norm_gains.shape[0]
    h = x
    a = rmsnorm_rows(h, norm_gains[0, 0], BF16)
    for i in range(depth):
        g = norm_gains[i]
        kind, j = i % N_MIXERS, i // N_MIXERS
        h, a = ffn_block(a, h, ffn_w_in[i, 0], ffn_w_out[i, 0], g[1], g[2])
        if kind == 0:
            y = nsa_mixer(a, pos, *[p[j] for p in nsa_params])
        elif kind == 1:
            y = dsa_mixer(a, pos, *[p[j] for p in dsa_params])
        else:
            y = s5_mixer(a, *[p[j] for p in s5_params])
        h, a = residual_norm(h, y, g[3], g[4], 1.0)
        h, a = cross_attn_block(a, h, mem, xa_mem_norm[i], xa_wq[i], xa_wkv[i], xa_wo[i], g[5], g[6])
        g_next = norm_gains[i + 1, 0] if i + 1 < depth else jnp.ones_like(g[0])
        h, a = ffn_block(a, h, ffn_w_in[i, 1], ffn_w_out[i, 1], g[7], g_next)
    return h


def kernel(x, mem, positions, norm_gains, ffn_w_in, ffn_w_out, xa_mem_norm, xa_wq, xa_wkv, xa_wo,
           nsa_w_in, nsa_cmp_pos, nsa_cmp_w1, nsa_cmp_w2, nsa_w_out,
           dsa_w_in, dsa_kv_norm, dsa_w_uk, dsa_w_uv, dsa_w_out,
           s5_lam_re, s5_lam_im, s5_log_step, s5_b_re, s5_b_im, s5_c_re, s5_c_im, s5_d, s5_w_glu):
    nsa_params = (nsa_w_in, nsa_cmp_pos, nsa_cmp_w1, nsa_cmp_w2, nsa_w_out)
    dsa_params = (dsa_w_in, dsa_kv_norm, dsa_w_uk, dsa_w_uv, dsa_w_out)
    s5_params = (s5_lam_re, s5_lam_im, s5_log_step, s5_b_re, s5_b_im, s5_c_re, s5_c_im, s5_d, s5_w_glu)
    outs = [_trunk(x[b], mem[b], positions[b], norm_gains, ffn_w_in, ffn_w_out, xa_mem_norm, xa_wq, xa_wkv, xa_wo,
                   nsa_params, dsa_params, s5_params) for b in range(x.shape[0])]
    return jnp.stack(outs, axis=0)
```

```python
import functools
import math

import jax
import jax.numpy as jnp
from jax import lax
from jax.experimental import pallas as pl
from jax.experimental.pallas import tpu as pltpu

F32 = jnp.float32
BF16 = jnp.bfloat16

NORM_EPS = 1e-6
ROPE_THETA = 10000.0
MASK_NEG = -1e30
Q_BLOCK = 128

V7X_VMEM_LIMIT_BYTES = 56 * 1024 * 1024
LANES = 128


def _params(*sem):
    return pltpu.CompilerParams(dimension_semantics=sem, vmem_limit_bytes=V7X_VMEM_LIMIT_BYTES)


def _resident(shape):
    nd = len(shape)
    return pl.BlockSpec(shape, lambda *_: (0,) * nd, pipeline_mode=pl.Buffered(1))


def _mm_kernel(a_ref, b_ref, o_ref):
    o_ref[...] = jnp.dot(a_ref[...], b_ref[...], preferred_element_type=F32).astype(o_ref.dtype)


def matmul(a, b, out_dtype, tm, tn):
    m, k = a.shape
    n = b.shape[1]
    tm, tn = min(tm, m), min(tn, n)
    assert m % tm == 0 and n % tn == 0, (m, n, tm, tn)
    return pl.pallas_call(
        _mm_kernel,
        grid=(m // tm, n // tn),
        in_specs=[pl.BlockSpec((tm, k), lambda i, j: (i, 0)),
                  pl.BlockSpec((k, tn), lambda i, j: (0, j))],
        out_specs=pl.BlockSpec((tm, tn), lambda i, j: (i, j)),
        out_shape=jax.ShapeDtypeStruct((m, n), out_dtype),
        compiler_params=_params("parallel", "arbitrary"),
        name="matmul",
    )(a, b)


def _gated_mm_kernel(a_ref, b1_ref, b2_ref, o_ref, *, gate_first):
    a = a_ref[...]
    y1 = jnp.dot(a, b1_ref[...], preferred_element_type=F32)
    y2 = jnp.dot(a, b2_ref[...], preferred_element_type=F32)
    if gate_first:
        o = (y1 * jax.nn.sigmoid(y1)) * y2
    else:
        o = y1 * jax.nn.sigmoid(y2)
    o_ref[...] = o.astype(o_ref.dtype)


def gated_matmul(a, w, out_dtype, tm, tn, gate_first):
    m, k = a.shape
    n = w.shape[1] // 2
    tm = min(tm, m)
    assert m % tm == 0 and n % tn == 0, (m, n, tm, tn)
    nb = n // tn
    return pl.pallas_call(
        functools.partial(_gated_mm_kernel, gate_first=gate_first),
        grid=(m // tm, nb),
        in_specs=[pl.BlockSpec((tm, k), lambda i, j: (i, 0)),
                  pl.BlockSpec((k, tn), lambda i, j: (0, j)),
                  pl.BlockSpec((k, tn), lambda i, j: (0, j + nb))],
        out_specs=pl.BlockSpec((tm, tn), lambda i, j: (i, j)),
        out_shape=jax.ShapeDtypeStruct((m, n), out_dtype),
        compiler_params=_params("parallel", "arbitrary"),
        name="gated_matmul",
    )(a, w, w)


def _rms(x, g):
    return x * lax.rsqrt(jnp.mean(x * x, axis=-1, keepdims=True) + NORM_EPS) * g


def _norm_kernel(x_ref, g_ref, o_ref):
    o_ref[...] = _rms(x_ref[...].astype(F32), g_ref[...]).astype(o_ref.dtype)


def rmsnorm_rows(x, g, out_dtype, tm=256):
    m, d = x.shape
    tm = min(tm, m)
    return pl.pallas_call(
        _norm_kernel,
        grid=(m // tm,),
        in_specs=[pl.BlockSpec((tm, d), lambda i: (i, 0)),
                  pl.BlockSpec((1, d), lambda i: (0, 0))],
        out_specs=pl.BlockSpec((tm, d), lambda i: (i, 0)),
        out_shape=jax.ShapeDtypeStruct((m, d), out_dtype),
        compiler_params=_params("parallel"),
        name="rmsnorm",
    )(x, g.reshape(1, d))


def _residual_norm_kernel(h_ref, y_ref, gpost_ref, gnext_ref, hout_ref, a_ref, *, coef):
    h = h_ref[...] + coef * _rms(y_ref[...].astype(F32), gpost_ref[...])
    hout_ref[...] = h
    a_ref[...] = _rms(h, gnext_ref[...]).astype(a_ref.dtype)


def residual_norm(h, y, g_post, g_next, coef, tm=256):
    m, d = h.shape
    tm = min(tm, m)
    row = pl.BlockSpec((tm, d), lambda i: (i, 0))
    vec = pl.BlockSpec((1, d), lambda i: (0, 0))
    return pl.pallas_call(
        functools.partial(_residual_norm_kernel, coef=coef),
        grid=(m // tm,),
        in_specs=[row, row, vec, vec],
        out_specs=[row, row],
        out_shape=[jax.ShapeDtypeStruct((m, d), F32), jax.ShapeDtypeStruct((m, d), BF16)],
        compiler_params=_params("parallel"),
        name="residual_norm",
    )(h, y, g_post.reshape(1, d), g_next.reshape(1, d))


XA_HEADS = 4
XA_HEAD_DIM = 128


def _xattn_kernel(a_ref, h_ref, wq_ref, k_ref, v_ref, wo_ref, gpost_ref, gnext_ref, hout_ref, anext_ref):
    q = jnp.dot(a_ref[...], wq_ref[...], preferred_element_type=F32)
    scale = XA_HEAD_DIM ** -0.5
    outs = []
    for hd in range(XA_HEADS):
        qh = q[:, hd * XA_HEAD_DIM:(hd + 1) * XA_HEAD_DIM].astype(BF16)
        s = lax.dot_general(qh, k_ref[hd], (((1,), (1,)), ((), ())), preferred_element_type=F32) * scale
        s = s - jnp.max(s, axis=-1, keepdims=True)
        p = jnp.exp(s)
        p = p / jnp.sum(p, axis=-1, keepdims=True)
        outs.append(jnp.dot(p.astype(BF16), v_ref[hd], preferred_element_type=F32))
    o = jnp.concatenate(outs, axis=-1).astype(BF16)
    y = jnp.dot(o, wo_ref[...], preferred_element_type=F32)
    h = h_ref[...] + _rms(y, gpost_ref[...])
    hout_ref[...] = h
    anext_ref[...] = _rms(h, gnext_ref[...]).astype(anext_ref.dtype)


def cross_attention_block(a, h, wq, k, v, wo, g_post, g_next, tm=256):
    t, d = h.shape
    tm = min(tm, t)
    hk = XA_HEADS * XA_HEAD_DIM
    mlen = k.shape[1]
    row = pl.BlockSpec((tm, d), lambda i: (i, 0))
    return pl.pallas_call(
        _xattn_kernel,
        grid=(t // tm,),
        in_specs=[row, row, _resident((d, hk)), _resident((XA_HEADS, mlen, XA_HEAD_DIM)),
                  _resident((XA_HEADS, mlen, XA_HEAD_DIM)), _resident((hk, d)),
                  _resident((1, d)), _resident((1, d))],
        out_specs=[row, row],
        out_shape=[jax.ShapeDtypeStruct((t, d), F32), jax.ShapeDtypeStruct((t, d), BF16)],
        compiler_params=_params("parallel"),
        name="cross_attention",
    )(a, h, wq, k, v, wo, g_post.reshape(1, d), g_next.reshape(1, d))


S5_GROUP = 16
S5_STATE = 64
S5_CHUNK = 128
S5_ROWS = 8
S5_BLOCK_GROUPS = 8
S5_BLOCK_IN = S5_BLOCK_GROUPS * S5_GROUP
S5_BLOCK_STATE = S5_BLOCK_GROUPS * S5_STATE


def _s5_kernel(u_ref, wb_ref, cs_ref, are_ref, aim_ref, d_ref, z_ref, bre, bim, sre, sim, *, chunk, n_blocks, lanes):
    blocks_per_row = lanes // S5_BLOCK_STATE
    tiles_per_block = S5_BLOCK_STATE // LANES
    tiles_per_pass = 4

    @pl.when(pl.program_id(0) == 0)
    def _():
        sre[...] = jnp.zeros_like(sre)
        sim[...] = jnp.zeros_like(sim)

    for b in range(n_blocks):
        r, tile0 = b // blocks_per_row, (b % blocks_per_row) * tiles_per_block
        bu = jnp.dot(u_ref[:, b * S5_BLOCK_IN:(b + 1) * S5_BLOCK_IN], wb_ref[b], preferred_element_type=F32)
        for j in range(tiles_per_block):
            bre[tile0 + j, pl.ds(r, chunk, stride=S5_ROWS), :] = bu[:, j * LANES:(j + 1) * LANES]
            bim[tile0 + j, pl.ds(r, chunk, stride=S5_ROWS), :] = bu[:, S5_BLOCK_STATE + j * LANES:S5_BLOCK_STATE + (j + 1) * LANES]

    for c0 in range(0, lanes // LANES, tiles_per_pass):
        sl = slice(c0, c0 + tiles_per_pass)
        a_re, a_im = are_ref[sl], aim_ref[sl]

        def step(t, carry, sl=sl, a_re=a_re, a_im=a_im):
            s_re, s_im = carry
            row = pl.multiple_of(t * S5_ROWS, S5_ROWS)
            n_re = a_re * s_re - a_im * s_im + bre[sl, pl.ds(row, S5_ROWS), :]
            n_im = a_re * s_im + a_im * s_re + bim[sl, pl.ds(row, S5_ROWS), :]
            bre[sl, pl.ds(row, S5_ROWS), :] = n_re
            bim[sl, pl.ds(row, S5_ROWS), :] = n_im
            return n_re, n_im

        s_re, s_im = lax.fori_loop(0, chunk, step, (sre[sl], sim[sl]))
        sre[sl] = s_re
        sim[sl] = s_im

    for b in range(n_blocks):
        r, tile0 = b // blocks_per_row, (b % blocks_per_row) * tiles_per_block
        parts = [bre[tile0 + j, pl.ds(r, chunk, stride=S5_ROWS), :] for j in range(tiles_per_block)]
        parts += [bim[tile0 + j, pl.ds(r, chunk, stride=S5_ROWS), :] for j in range(tiles_per_block)]
        hist = jnp.concatenate(parts, axis=-1).astype(BF16)
        cols = slice(b * S5_BLOCK_IN, (b + 1) * S5_BLOCK_IN)
        y = jnp.dot(hist, cs_ref[b], preferred_element_type=F32) + d_ref[:, cols] * u_ref[:, cols].astype(F32)
        z_ref[:, cols] = jax.nn.gelu(y).astype(z_ref.dtype)


def _block_diag(w, n_blocks):
    g, a, b = w.shape
    eye = jnp.eye(S5_BLOCK_GROUPS, dtype=w.dtype)
    w = w.reshape(n_blocks, S5_BLOCK_GROUPS, a, b)
    return jnp.einsum('ngab,gh->ngahb', w, eye).reshape(n_blocks, S5_BLOCK_GROUPS * a, S5_BLOCK_GROUPS * b)


def _state_tiles(x, lanes):
    return jnp.transpose(x.reshape(S5_ROWS, lanes // LANES, LANES), (1, 0, 2))


def s5_scan(u, lam_re, lam_im, log_step, b_re, b_im, c_re, c_im, d_skip):
    t, d = u.shape
    groups = d // S5_GROUP
    n_blocks = groups // S5_BLOCK_GROUPS
    lanes = groups * S5_STATE // S5_ROWS
    chunk = min(S5_CHUNK, t)
    dt = jnp.exp(log_step.astype(F32))[:, None]
    lr = jnp.minimum(lam_re.astype(F32), -1e-4)
    li = lam_im.astype(F32)
    mag = jnp.exp(lr * dt)
    a_re = mag * jnp.cos(li * dt)
    a_im = mag * jnp.sin(li * dt)
    den = lr * lr + li * li
    coef_re = ((a_re - 1.0) * lr + a_im * li) / den
    coef_im = (a_im * lr - (a_re - 1.0) * li) / den
    br, bi_ = b_re.astype(F32), b_im.astype(F32)
    bb_re = coef_re[..., None] * br - coef_im[..., None] * bi_
    bb_im = coef_re[..., None] * bi_ + coef_im[..., None] * br
    wb = jnp.concatenate([_block_diag(jnp.swapaxes(bb_re, 1, 2), n_blocks),
                          _block_diag(jnp.swapaxes(bb_im, 1, 2), n_blocks)], axis=-1).astype(BF16)
    cs = jnp.concatenate([_block_diag(jnp.swapaxes(c_re.astype(F32), 1, 2), n_blocks),
                          _block_diag(-jnp.swapaxes(c_im.astype(F32), 1, 2), n_blocks)], axis=1).astype(BF16)
    return pl.pallas_call(
        functools.partial(_s5_kernel, chunk=chunk, n_blocks=n_blocks, lanes=lanes),
        grid=(t // chunk,),
        in_specs=[pl.BlockSpec((chunk, d), lambda i: (i, 0)),
                  _resident(wb.shape), _resident(cs.shape),
                  _resident((lanes // LANES, S5_ROWS, LANES)), _resident((lanes // LANES, S5_ROWS, LANES)),
                  _resident((1, d))],
        out_specs=pl.BlockSpec((chunk, d), lambda i: (i, 0)),
        out_shape=jax.ShapeDtypeStruct((t, d), BF16),
        scratch_shapes=[pltpu.VMEM((lanes // LANES, chunk * S5_ROWS, LANES), F32),
                        pltpu.VMEM((lanes // LANES, chunk * S5_ROWS, LANES), F32),
                        pltpu.VMEM((lanes // LANES, S5_ROWS, LANES), F32),
                        pltpu.VMEM((lanes // LANES, S5_ROWS, LANES), F32)],
        compiler_params=_params("arbitrary"),
        name="s5_scan",
    )(u, wb, cs, _state_tiles(a_re, lanes), _state_tiles(a_im, lanes), d_skip.astype(F32).reshape(1, d))


def ffn_block(a, h, w_in, w_out, g_post, g_next):
    act = gated_matmul(a, w_in.astype(BF16), BF16, tm=1024, tn=512, gate_first=True)
    y = matmul(act, w_out.astype(BF16), F32, tm=1024, tn=512)
    return residual_norm(h, y, g_post, g_next, 0.5)


def cross_attn_block(a, h, mem, mem_norm, wq, wkv, wo, g_post, g_next):
    mlen = mem.shape[0]
    memn = rmsnorm_rows(mem, mem_norm, BF16)
    kv = matmul(memn, wkv.astype(BF16), BF16, tm=256, tn=512).reshape(mlen, 2, XA_HEADS, XA_HEAD_DIM)
    k = jnp.transpose(kv[:, 0], (1, 0, 2))
    v = jnp.transpose(kv[:, 1], (1, 0, 2))
    return cross_attention_block(a, h, wq.astype(BF16), k, v, wo.astype(BF16), g_post, g_next)


def s5_mixer(a, lam_re, lam_im, log_step, b_re, b_im, c_re, c_im, d_skip, w_glu):
    z = s5_scan(a, lam_re, lam_im, log_step, b_re, b_im, c_re, c_im, d_skip)
    return gated_matmul(z, w_glu.astype(BF16), F32, tm=1024, tn=512, gate_first=False)


def rope_tables(pos, d):
    inv = ROPE_THETA ** (-jnp.arange(0, d, 2, dtype=F32) / d)
    ang = pos.astype(F32)[:, None] * inv
    cos, sin = jnp.cos(ang), jnp.sin(ang)
    return jnp.concatenate([cos, cos], axis=-1), jnp.concatenate([-sin, sin], axis=-1)


def _rope(x, cosf, sinf):
    return x * cosf + pltpu.roll(x, x.shape[-1] // 2, axis=1) * sinf


def _dot_nt(a, b):
    return lax.dot_general(a, b, (((1,), (1,)), ((), ())), preferred_element_type=F32)


def _online_softmax_step(q, k_blk, v_blk, bias, m_ref, l_ref, acc_ref, chain, heads, qb):
    s = _dot_nt(q, k_blk)
    kb = s.shape[-1]
    s = (s.reshape(heads, qb, kb) + bias[None]).reshape(heads * qb, kb)
    m_prev = m_ref[chain]
    m_new = jnp.maximum(m_prev, jnp.max(s, axis=-1, keepdims=True))
    alpha = jnp.exp(m_prev - m_new)
    p = jnp.exp(s - m_new)
    l_ref[chain] = alpha * l_ref[chain] + jnp.sum(p, axis=-1, keepdims=True)
    acc_ref[chain] = alpha * acc_ref[chain] + jnp.dot(p.astype(BF16), v_blk, preferred_element_type=F32)
    m_ref[chain] = m_new


def _reset_softmax_state(m_ref, l_ref, acc_ref):
    m_ref[...] = jnp.full_like(m_ref, MASK_NEG)
    l_ref[...] = jnp.zeros_like(l_ref)
    acc_ref[...] = jnp.zeros_like(acc_ref)


NSA_HEADS = 32
NSA_GROUPS = 4
NSA_HPG = NSA_HEADS // NSA_GROUPS
NSA_DH = 128
NSA_KV_WIDTH = NSA_GROUPS * NSA_DH
NSA_CMP_STRIDE = 16
NSA_CMP_LEN = 2 * NSA_CMP_STRIDE
NSA_SEL_BLOCK = 64
NSA_CMP_PER_SEL = NSA_SEL_BLOCK // NSA_CMP_STRIDE
NSA_N_SEL = 16
NSA_WINDOW = 512
NSA_FORCE = 1e4
NSA_GATE_COLS = 3 * NSA_HPG
NSA_KEY_CHUNK = 1024
NSA_CHAINS = 2


def _nsa_prep_kernel(kc_ref, vc_ref, ks_ref, vs_ref, kw_ref, vw_ref, cos_ref, sin_ref, okvc, oks, ovs, okw, ovw):
    cosf, sinf = cos_ref[...], sin_ref[...]
    for g in range(NSA_GROUPS):
        sl = slice(g * NSA_DH, (g + 1) * NSA_DH)
        okvc[0, g] = kc_ref[:, sl]
        okvc[1, g] = vc_ref[:, sl]
        ovs[g] = vs_ref[:, sl]
        ovw[g] = vw_ref[:, sl]
        oks[g] = _rope(ks_ref[:, sl].astype(F32), cosf, sinf).astype(BF16)
        okw[g] = _rope(kw_ref[:, sl].astype(F32), cosf, sinf).astype(BF16)


def nsa_prep(qkv, cosf, sinf, tb=512):
    t = qkv.shape[0]
    tb = min(tb, t)
    first = NSA_HEADS * NSA_DH // NSA_KV_WIDTH
    piece = lambda p: pl.BlockSpec((tb, NSA_KV_WIDTH), lambda i, p=p: (i, first + p))
    tab = pl.BlockSpec((tb, NSA_DH), lambda i: (i, 0))
    grouped = pl.BlockSpec((NSA_GROUPS, tb, NSA_DH), lambda i: (0, i, 0))
    gshape = jax.ShapeDtypeStruct((NSA_GROUPS, t, NSA_DH), BF16)
    return pl.pallas_call(
        _nsa_prep_kernel,
        grid=(t // tb,),
        in_specs=[piece(p) for p in range(6)] + [tab, tab],
        out_specs=[pl.BlockSpec((2, NSA_GROUPS, tb, NSA_DH), lambda i: (0, 0, i, 0)), grouped, grouped, grouped, grouped],
        out_shape=[jax.ShapeDtypeStruct((2, NSA_GROUPS, t, NSA_DH), BF16), gshape, gshape, gshape, gshape],
        compiler_params=_params("parallel"),
        name="nsa_prep",
    )(qkv, qkv, qkv, qkv, qkv, qkv, cosf, sinf)


def _nsa_cmp_kernel(x_ref, pos_ref, w1_ref, w2_ref, o_ref, *, nc):
    half = NSA_CMP_STRIDE * NSA_DH
    x = x_ref[0, 0].astype(F32)
    pos_a, pos_b = pos_ref[0, 0:1, :], pos_ref[0, 1:2, :]
    w1a, w1b = w1_ref[0, :half, :], w1_ref[0, half:, :]
    y1 = jnp.dot((x + pos_a).astype(BF16), w1a, preferred_element_type=F32)
    y2 = jnp.dot((x + pos_b).astype(BF16), w1b, preferred_element_type=F32)
    y_pad = jnp.dot(jnp.broadcast_to(pos_b, (8, half)).astype(BF16), w1b, preferred_element_type=F32)[0:1]
    rows = lax.broadcasted_iota(jnp.int32, (nc, 1), 0)
    y2_next = jnp.where(rows == nc - 1, y_pad, pltpu.roll(y2, nc - 1, axis=0))
    pre = y1 + y2_next
    hid = (pre * jax.nn.sigmoid(pre)).astype(BF16)
    o_ref[0, 0] = jnp.dot(hid, w2_ref[0], preferred_element_type=F32).astype(o_ref.dtype)


def nsa_compress(kvc, cmp_pos, cmp_w1, cmp_w2):
    _, g, t, dh = kvc.shape
    nc = t // NSA_CMP_STRIDE
    row = NSA_CMP_STRIDE * dh
    x = kvc.reshape(2, g, nc, row)
    return pl.pallas_call(
        functools.partial(_nsa_cmp_kernel, nc=nc),
        grid=(2, g),
        in_specs=[pl.BlockSpec((1, 1, nc, row), lambda kv, gi: (kv, gi, 0, 0)),
                  pl.BlockSpec((1, 2, row), lambda kv, gi: (kv, 0, 0)),
                  pl.BlockSpec((1, 2 * row, dh), lambda kv, gi: (kv, 0, 0)),
                  pl.BlockSpec((1, dh, dh), lambda kv, gi: (kv, 0, 0))],
        out_specs=pl.BlockSpec((1, 1, nc, dh), lambda kv, gi: (kv, gi, 0, 0)),
        out_shape=jax.ShapeDtypeStruct((2, g, nc, dh), BF16),
        compiler_params=_params("parallel", "parallel"),
        name="nsa_compress",
    )(x, cmp_pos.astype(F32).reshape(2, 2, row), cmp_w1.astype(BF16), cmp_w2.astype(BF16))


def _nsa_attn_kernel(q_ref, cos_ref, sin_ref, gate_ref, kc_ref, vc_ref, ks_ref, vs_ref, kw_ref, vw_ref, o_ref,
                     m_ref, l_ref, acc_ref, *, nc, ns, n_sel, kchunk):
    bi = pl.program_id(1)
    qb, hpg, dh = Q_BLOCK, NSA_HPG, NSA_DH
    start = bi * qb
    tq = start + lax.broadcasted_iota(jnp.int32, (qb, 1), 0)
    scale = dh ** -0.5
    cosf, sinf = cos_ref[...], sin_ref[...]
    q_un, q_rot = [], []
    for h in range(hpg):
        qh = q_ref[:, h * dh:(h + 1) * dh].astype(F32)
        q_un.append((qh * scale).astype(BF16))
        q_rot.append((_rope(qh, cosf, sinf) * scale).astype(BF16))
    q_un = jnp.concatenate(q_un, axis=0)
    q_rot = jnp.concatenate(q_rot, axis=0)

    s3 = _dot_nt(q_un, kc_ref[0]).reshape(hpg, qb, nc)
    lane = lax.broadcasted_iota(jnp.int32, (1, nc), 1)
    cmp_idx = (lane % ns) * NSA_CMP_PER_SEL + lane // ns
    cmask = (cmp_idx * NSA_CMP_STRIDE + (NSA_CMP_LEN - 1) <= tq)[None]
    s3 = jnp.where(cmask, s3, MASK_NEG)
    p = jnp.where(cmask, jnp.exp(s3 - jnp.max(s3, axis=-1, keepdims=True)), 0.0)
    p = p * (1.0 / jnp.maximum(jnp.sum(p, axis=-1, keepdims=True), 1e-30))
    o_c = jnp.dot(p.reshape(hpg * qb, nc).astype(BF16), vc_ref[0], preferred_element_type=F32)
    imp_c = jnp.sum(p, axis=0)
    imp = imp_c[:, 0:ns]
    for r in range(1, NSA_CMP_PER_SEL):
        imp = imp + imp_c[:, r * ns:(r + 1) * ns]

    ids = lax.broadcasted_iota(jnp.int32, (1, ns), 1)
    ids_f = ids.astype(F32)
    cur = tq // NSA_SEL_BLOCK
    valid = ids * NSA_SEL_BLOCK <= tq
    forced = (ids == 0) | (ids == cur) | (ids == cur - 1)
    val = jnp.where(forced, NSA_FORCE, jnp.where(valid, imp, -1.0))
    sel = jnp.zeros((qb, ns), F32)
    for _ in range(n_sel):
        mx = jnp.max(val, axis=-1, keepdims=True)
        first = jnp.min(jnp.where(val == mx, ids_f, float(ns)), axis=-1, keepdims=True)
        pick = ids_f == first
        sel = jnp.where(pick, 1.0, sel)
        val = jnp.where(pick, -3e38, val)
    sel_b = sel.astype(BF16)

    _reset_softmax_state(m_ref, l_ref, acc_ref)
    blk_row = lax.broadcasted_iota(jnp.int32, (ns, kchunk), 0)
    blk_of_key = lax.broadcasted_iota(jnp.int32, (ns, kchunk), 1) // NSA_SEL_BLOCK
    key_lane = lax.broadcasted_iota(jnp.int32, (1, kchunk), 1)

    chain_heads = hpg // NSA_CHAINS
    chain_rows = chain_heads * qb
    q_chains = [q_rot[c * chain_rows:(c + 1) * chain_rows] for c in range(NSA_CHAINS)]

    def sel_body(ci, carry):
        k0 = pl.multiple_of(ci * kchunk, kchunk)
        expand = jnp.where(blk_row == blk_of_key + k0 // NSA_SEL_BLOCK, 1.0, 0.0).astype(BF16)
        selx = jnp.dot(sel_b, expand, preferred_element_type=F32)
        bias = jnp.where((selx > 0.5) & (key_lane + k0 <= tq), 0.0, MASK_NEG)
        k_blk, v_blk = ks_ref[0, pl.ds(k0, kchunk), :], vs_ref[0, pl.ds(k0, kchunk), :]
        for c in range(NSA_CHAINS):
            _online_softmax_step(q_chains[c], k_blk, v_blk, bias, m_ref, l_ref, acc_ref, c, chain_heads, qb)
        return carry

    lax.fori_loop(0, (start + qb + kchunk - 1) // kchunk, sel_body, 0)
    o_s = (acc_ref[...] / l_ref[...]).reshape(hpg * qb, dh)

    _reset_softmax_state(m_ref, l_ref, acc_ref)
    n_wblk = NSA_WINDOW // qb + 1
    wlane = lax.broadcasted_iota(jnp.int32, (1, qb), 1)

    def win_body(j, carry):
        k0 = pl.multiple_of((bi - (n_wblk - 1) + j) * qb, qb)
        diff = tq - (wlane + k0)
        bias = jnp.where((diff >= 0) & (diff < NSA_WINDOW), 0.0, MASK_NEG)
        k_blk, v_blk = kw_ref[0, pl.ds(k0, qb), :], vw_ref[0, pl.ds(k0, qb), :]
        for c in range(NSA_CHAINS):
            _online_softmax_step(q_chains[c], k_blk, v_blk, bias, m_ref, l_ref, acc_ref, c, chain_heads, qb)
        return carry

    lax.fori_loop(jnp.maximum(0, n_wblk - 1 - bi), n_wblk, win_body, 0)
    o_w = (acc_ref[...] / l_ref[...]).reshape(hpg * qb, dh)

    gs = jax.nn.sigmoid(gate_ref[0])
    for h in range(hpg):
        rows = slice(h * qb, (h + 1) * qb)
        o = (gs[:, 3 * h:3 * h + 1] * o_c[rows] + gs[:, 3 * h + 1:3 * h + 2] * o_s[rows]
             + gs[:, 3 * h + 2:3 * h + 3] * o_w[rows])
        o_ref[:, h * dh:(h + 1) * dh] = o.astype(o_ref.dtype)


def nsa_attention(qkv, cosf, sinf, gate, k_cmp, v_cmp, ks, vs, kw, vw):
    g, t, dh = ks.shape
    nc, ns = t // NSA_CMP_STRIDE, t // NSA_SEL_BLOCK
    nb = t // Q_BLOCK
    rows = NSA_HPG * Q_BLOCK
    kchunk = min(NSA_KEY_CHUNK, t)
    whole = lambda n: pl.BlockSpec((1, n, dh), lambda gi, i: (gi, 0, 0))
    kernel = functools.partial(_nsa_attn_kernel, nc=nc, ns=ns, n_sel=min(NSA_N_SEL, ns), kchunk=kchunk)
    return pl.pallas_call(
        kernel,
        grid=(g, nb),
        in_specs=[pl.BlockSpec((Q_BLOCK, NSA_HPG * dh), lambda gi, i: (i, gi)),
                  pl.BlockSpec((Q_BLOCK, dh), lambda gi, i: (i, 0)),
                  pl.BlockSpec((Q_BLOCK, dh), lambda gi, i: (i, 0)),
                  pl.BlockSpec((1, Q_BLOCK, LANES), lambda gi, i: (gi, i, 0)),
                  whole(nc), whole(nc), whole(t), whole(t), whole(t), whole(t)],
        out_specs=pl.BlockSpec((Q_BLOCK, NSA_HPG * dh), lambda gi, i: (i, gi)),
        out_shape=jax.ShapeDtypeStruct((t, NSA_HEADS * dh), BF16),
        scratch_shapes=[pltpu.VMEM((NSA_CHAINS, rows // NSA_CHAINS, 1), F32),
                        pltpu.VMEM((NSA_CHAINS, rows // NSA_CHAINS, 1), F32),
                        pltpu.VMEM((NSA_CHAINS, rows // NSA_CHAINS, dh), F32)],
        compiler_params=_params("parallel", "arbitrary"),
        name="nsa_attention",
    )(qkv, cosf, sinf, gate, k_cmp, v_cmp, ks, vs, kw, vw)


def _pad_cols(w, multiple):
    n = w.shape[1]
    return jnp.pad(w, ((0, 0), (0, -n % multiple)))


def nsa_mixer(a, pos, w_in, cmp_pos, cmp_w1, cmp_w2, w_out):
    t = a.shape[0]
    g, dh = NSA_GROUPS, NSA_DH
    qkv = matmul(a, _pad_cols(w_in.astype(BF16), 256), BF16, tm=1024, tn=256)
    cosf, sinf = rope_tables(pos, dh)
    kvc, ks, vs, kw, vw = nsa_prep(qkv, cosf, sinf)
    cmp = nsa_compress(kvc, cmp_pos, cmp_w1, cmp_w2)
    nc, ns = t // NSA_CMP_STRIDE, t // NSA_SEL_BLOCK
    cmp = cmp.reshape(2, g, ns, NSA_CMP_PER_SEL, dh).transpose(0, 1, 3, 2, 4).reshape(2, g, nc, dh)
    gate0 = NSA_HEADS * dh + 6 * NSA_KV_WIDTH
    gate = qkv[:, gate0:gate0 + NSA_HEADS * 3].astype(F32).reshape(t, g, NSA_GATE_COLS).transpose(1, 0, 2)
    gate = jnp.pad(gate, ((0, 0), (0, 0), (0, LANES - NSA_GATE_COLS)))
    o = nsa_attention(qkv, cosf, sinf, gate, cmp[0], cmp[1], ks, vs, kw, vw)
    return matmul(o, w_out.astype(BF16), F32, tm=1024, tn=512)


DSA_HEADS = 32
DSA_NOPE = 128
DSA_ROPE = 64
DSA_V = 128
DSA_LATENT = 512
DSA_IDX_HEADS = 32
DSA_IDX_DIM = 128
DSA_TOPK_MAX = 256
DSA_KEY_CHUNK = 512
DSA_Q_TILE = 512
DSA_HEAD_GROUP = 2
INT32_MIN = -2 ** 31
DSA_QN0 = 0
DSA_QIDX0 = DSA_QN0 + DSA_HEADS * DSA_NOPE
DSA_QPE0 = DSA_QIDX0 + DSA_IDX_HEADS * DSA_IDX_DIM
DSA_CKV0 = DSA_QPE0 + DSA_HEADS * DSA_ROPE
DSA_KPE0 = DSA_CKV0 + DSA_LATENT
DSA_KIDX0 = DSA_KPE0 + LANES
DSA_WIDX0 = DSA_KIDX0 + DSA_IDX_DIM


def _dsa_w_in_layout(w_in):
    h = DSA_HEADS
    sizes = [h * DSA_NOPE, h * DSA_ROPE, DSA_LATENT, DSA_ROPE, DSA_IDX_HEADS * DSA_IDX_DIM, DSA_IDX_DIM, DSA_IDX_HEADS]
    offs = [0]
    for s in sizes:
        offs.append(offs[-1] + s)
    q_nope, q_pe, c_kv, k_pe, q_idx, k_idx, w_idx = [w_in[:, offs[i]:offs[i + 1]] for i in range(7)]
    scale = (DSA_NOPE + DSA_ROPE) ** -0.5
    idx_scale = (DSA_IDX_HEADS * DSA_IDX_DIM) ** -0.5
    pad = lambda x: jnp.pad(x, ((0, 0), (0, LANES - x.shape[1])))
    w = jnp.concatenate([q_nope * scale, q_idx, q_pe * scale, c_kv, pad(k_pe), k_idx, pad(w_idx * idx_scale)], axis=1)
    return _pad_cols(w.astype(BF16), 256)


def _rope64_pairs(x, cosf, sinf):
    lane = lax.broadcasted_iota(jnp.int32, x.shape, 1)
    swapped = jnp.where(lane % DSA_ROPE < DSA_ROPE // 2, pltpu.roll(x, LANES - DSA_ROPE // 2, axis=1),
                        pltpu.roll(x, DSA_ROPE // 2, axis=1))
    return x * cosf + swapped * sinf


def _dsa_prep_kernel(qidx_ref, qpe_ref, ckv_ref, kpe_ref, kidx_ref, cos_ref, sin_ref, cos64_ref, sin64_ref, g_ref,
                     oqidx, oqpe, ockv, okpe, okidx):
    cosf, sinf = cos_ref[...], sin_ref[...]
    cos64, sin64 = cos64_ref[...], sin64_ref[...]
    low = lax.broadcasted_iota(jnp.int32, cosf.shape, 1) < DSA_ROPE
    for h in range(DSA_IDX_HEADS):
        sl = slice(h * DSA_IDX_DIM, (h + 1) * DSA_IDX_DIM)
        oqidx[:, sl] = _rope(qidx_ref[:, sl].astype(F32), cosf, sinf).astype(BF16)
    okidx[...] = _rope(kidx_ref[...].astype(F32), cosf, sinf).astype(BF16)
    for j in range(DSA_HEADS // 2):
        r = _rope64_pairs(qpe_ref[:, j * LANES:(j + 1) * LANES].astype(F32), cos64, sin64)
        oqpe[:, (2 * j) * LANES:(2 * j + 1) * LANES] = jnp.where(low, r, 0.0).astype(BF16)
        oqpe[:, (2 * j + 1) * LANES:(2 * j + 2) * LANES] = jnp.where(low, pltpu.roll(r, DSA_ROPE, axis=1), 0.0).astype(BF16)
    okpe[...] = jnp.where(low, _rope64_pairs(kpe_ref[...].astype(F32), cos64, sin64), 0.0).astype(BF16)
    ockv[...] = _rms(ckv_ref[...].astype(F32), g_ref[...]).astype(BF16)


def dsa_prep(qkv, cosf, sinf, cos64, sin64, kv_norm, tb=256):
    t = qkv.shape[0]
    tb = min(tb, t)
    col = lambda width, off: pl.BlockSpec((tb, width), lambda i: (i, off // width))
    tab = pl.BlockSpec((tb, LANES), lambda i: (i, 0))
    wide = DSA_IDX_HEADS * DSA_IDX_DIM
    out = lambda width: pl.BlockSpec((tb, width), lambda i: (i, 0))
    return pl.pallas_call(
        _dsa_prep_kernel,
        grid=(t // tb,),
        in_specs=[col(wide, DSA_QIDX0), col(DSA_HEADS * DSA_ROPE, DSA_QPE0), col(DSA_LATENT, DSA_CKV0),
                  col(LANES, DSA_KPE0), col(LANES, DSA_KIDX0), tab, tab, tab, tab,
                  pl.BlockSpec((1, DSA_LATENT), lambda i: (0, 0))],
        out_specs=[out(wide), out(DSA_HEADS * LANES), out(DSA_LATENT), out(LANES), out(LANES)],
        out_shape=[jax.ShapeDtypeStruct((t, wide), BF16), jax.ShapeDtypeStruct((t, DSA_HEADS * LANES), BF16),
                   jax.ShapeDtypeStruct((t, DSA_LATENT), BF16), jax.ShapeDtypeStruct((t, LANES), BF16),
                   jax.ShapeDtypeStruct((t, LANES), BF16)],
        compiler_params=_params("parallel"),
        name="dsa_prep",
    )(qkv, qkv, qkv, qkv, qkv, cosf, sinf, cos64, sin64, kv_norm.astype(F32).reshape(1, DSA_LATENT))


def _dsa_index_kernel(qi_ref, wi_ref, kidx_ref, mask_ref, keys_ref, *, topk, kchunk, idx_bits, heads_per_dot):
    bi = pl.program_id(0)
    qb = Q_BLOCK
    start = bi * qb
    tq = start + lax.broadcasted_iota(jnp.int32, (qb, 1), 0)
    n_chunks = (start + qb + kchunk - 1) // kchunk
    w = wi_ref[...].astype(F32)
    lane = lax.broadcasted_iota(jnp.int32, (1, kchunk), 1)

    def score_body(ci, carry):
        k0 = pl.multiple_of(ci * kchunk, kchunk)
        k_blk = kidx_ref[pl.ds(k0, kchunk), :]
        acc = jnp.zeros((qb, kchunk), F32)
        for h0 in range(0, DSA_IDX_HEADS, heads_per_dot):
            qs = jnp.concatenate([qi_ref[:, h * DSA_IDX_DIM:(h + 1) * DSA_IDX_DIM]
                                  for h in range(h0, h0 + heads_per_dot)], axis=0)
            logits = _dot_nt(qs, k_blk)
            for j in range(heads_per_dot):
                acc = acc + w[:, h0 + j:h0 + j + 1] * jnp.maximum(logits[j * qb:(j + 1) * qb], 0.0)
        bits = pltpu.bitcast(acc, jnp.int32)
        key = jnp.where(bits < 0, bits ^ 0x7FFFFFFF, bits)
        keys_ref[:, pl.ds(k0, kchunk)] = jnp.where(lane + k0 <= tq, key, INT32_MIN)
        return carry

    lax.fori_loop(0, n_chunks, score_body, 0)

    def count(pred):
        def body(ci, cnt):
            k0 = pl.multiple_of(ci * kchunk, kchunk)
            return cnt + jnp.where(pred(keys_ref[:, pl.ds(k0, kchunk)], lane + k0), 1.0, 0.0)
        cnt = lax.fori_loop(0, n_chunks, body, jnp.zeros((qb, kchunk), F32))
        return jnp.sum(cnt, axis=-1, keepdims=True)

    def value_bit(i, thr):
        cand = thr + lax.shift_left(jnp.int32(1), 31 - i)
        return jnp.where(count(lambda key, idx: key >= cand) >= topk, cand, thr)

    thr = lax.fori_loop(0, 32, value_bit, jnp.full((qb, 1), INT32_MIN, jnp.int32))
    n_gt = count(lambda key, idx: key > thr)
    n_eq = count(lambda key, idx: key == thr)
    need = topk - n_gt

    def index_search():
        def index_bit(i, lim):
            cand = lim + lax.shift_left(jnp.int32(1), idx_bits - 1 - i)
            return jnp.where(count(lambda key, idx: (key == thr) & (idx < cand)) < need, cand, lim)
        return lax.fori_loop(0, idx_bits, index_bit, jnp.zeros((qb, 1), jnp.int32))

    all_ties = jnp.full((qb, 1), (1 << idx_bits) - 1, jnp.int32)
    lim = lax.cond(jnp.max(n_eq - need) > 0.0, index_search, lambda: all_ties)

    def pack_body(ci, m):
        k0 = pl.multiple_of(ci * kchunk, kchunk)
        key = keys_ref[:, pl.ds(k0, kchunk)]
        idx = lane + k0
        chosen = ((key > thr) | ((key == thr) & (idx <= lim))) & (idx <= tq)
        return m | jnp.where(chosen, lax.shift_left(jnp.int32(1), ci), 0)

    mask_ref[...] = lax.fori_loop(0, n_chunks, pack_body, jnp.zeros((qb, kchunk), jnp.int32))


def dsa_index(qidx, qkv, kidx, topk):
    t = qidx.shape[0]
    kchunk = min(DSA_KEY_CHUNK, t)
    assert t // kchunk <= 32, "selection bits are packed into one int32 word per key lane"
    idx_bits = max(1, (t - 1).bit_length())
    kernel = functools.partial(_dsa_index_kernel, topk=float(topk), kchunk=kchunk, idx_bits=idx_bits, heads_per_dot=8)
    return pl.pallas_call(
        kernel,
        grid=(t // Q_BLOCK,),
        in_specs=[pl.BlockSpec((Q_BLOCK, DSA_IDX_HEADS * DSA_IDX_DIM), lambda i: (i, 0)),
                  pl.BlockSpec((Q_BLOCK, LANES), lambda i: (i, DSA_WIDX0 // LANES)),
                  _resident((t, DSA_IDX_DIM))],
        out_specs=pl.BlockSpec((Q_BLOCK, kchunk), lambda i: (i, 0)),
        out_shape=jax.ShapeDtypeStruct((t, kchunk), jnp.int32),
        scratch_shapes=[pltpu.VMEM((Q_BLOCK, t), jnp.int32)],
        compiler_params=_params("parallel"),
        name="dsa_index",
    )(qidx, qkv, kidx)


def _dsa_attn_kernel(qn_ref, qpe_ref, kn_ref, kpe_ref, v_ref, mask_ref, o_ref, m_ref, l_ref, acc_ref, *, kchunk):
    j = pl.program_id(1)
    tq = qn_ref.shape[0]
    head = lambda c: slice(c * LANES, (c + 1) * LANES)
    qs = [jnp.concatenate([qn_ref[:, head(c)], qpe_ref[:, head(c)]], axis=1) for c in range(DSA_HEAD_GROUP)]
    mbits = mask_ref[...]
    _reset_softmax_state(m_ref, l_ref, acc_ref)

    def body(ci, carry):
        k0 = pl.multiple_of(ci * kchunk, kchunk)
        bit = lax.shift_right_logical(mbits, ci) & 1
        bias = jnp.where(bit == 1, 0.0, MASK_NEG)
        k_pe = kpe_ref[pl.ds(k0, kchunk), :]
        for c in range(DSA_HEAD_GROUP):
            k_cat = jnp.concatenate([kn_ref[pl.ds(k0, kchunk), head(c)], k_pe], axis=1)
            _online_softmax_step(qs[c], k_cat, v_ref[pl.ds(k0, kchunk), head(c)], bias, m_ref, l_ref, acc_ref, c, 1, tq)
        return carry

    lax.fori_loop(0, ((j + 1) * tq + kchunk - 1) // kchunk, body, 0)
    for c in range(DSA_HEAD_GROUP):
        o_ref[:, head(c)] = (acc_ref[c] / l_ref[c]).astype(o_ref.dtype)


def dsa_attention(qkv, qpe, kv, kpe, mask):
    t = qkv.shape[0]
    kchunk = mask.shape[1]
    tq = min(DSA_Q_TILE, t)
    h = DSA_HEADS
    n_groups = h // DSA_HEAD_GROUP
    width = DSA_HEAD_GROUP * LANES
    return pl.pallas_call(
        functools.partial(_dsa_attn_kernel, kchunk=kchunk),
        grid=(n_groups, t // tq),
        in_specs=[pl.BlockSpec((tq, width), lambda hi, j: (j, hi)),
                  pl.BlockSpec((tq, width), lambda hi, j: (j, hi)),
                  pl.BlockSpec((t, width), lambda hi, j: (0, hi)),
                  _resident((t, LANES)),
                  pl.BlockSpec((t, width), lambda hi, j: (0, n_groups + hi)),
                  pl.BlockSpec((tq, kchunk), lambda hi, j: (j, 0))],
        out_specs=pl.BlockSpec((tq, width), lambda hi, j: (j, hi)),
        out_shape=jax.ShapeDtypeStruct((t, h * DSA_V), BF16),
        scratch_shapes=[pltpu.VMEM((DSA_HEAD_GROUP, tq, 1), F32), pltpu.VMEM((DSA_HEAD_GROUP, tq, 1), F32),
                        pltpu.VMEM((DSA_HEAD_GROUP, tq, DSA_V), F32)],
        compiler_params=_params("parallel", "arbitrary"),
        name="dsa_attention",
    )(qkv, qpe, kv, kpe, kv, mask)


def dsa_mixer(a, pos, w_in, kv_norm, w_uk, w_uv, w_out):
    t = a.shape[0]
    qkv = matmul(a, _dsa_w_in_layout(w_in), BF16, tm=1024, tn=256)
    cosf, sinf = rope_tables(pos, DSA_IDX_DIM)
    cos64, sin64 = rope_tables(pos, DSA_ROPE)
    cos64, sin64 = jnp.tile(cos64, (1, 2)), jnp.tile(sin64, (1, 2))
    qidx, qpe, ckv, kpe, kidx = dsa_prep(qkv, cosf, sinf, cos64, sin64, kv_norm)
    w_kv = jnp.concatenate([jnp.transpose(w_uk, (1, 0, 2)).reshape(DSA_LATENT, DSA_HEADS * DSA_NOPE),
                            jnp.transpose(w_uv, (1, 0, 2)).reshape(DSA_LATENT, DSA_HEADS * DSA_V)], axis=1)
    kv = matmul(ckv, w_kv.astype(BF16), BF16, tm=1024, tn=512)
    mask = dsa_index(qidx, qkv, kidx, min(DSA_TOPK_MAX, t // 4))
    o = dsa_attention(qkv, qpe, kv, kpe, mask)
    return matmul(o, w_out.astype(BF16), F32, tm=1024, tn=512)


N_MIXERS = 3


def _trunk(x, mem, pos, norm_gains, ffn_w_in, ffn_w_out, xa_mem_norm, xa_wq, xa_wkv, xa_wo,
           nsa_params, dsa_params, s5_params):
    depth = norm_gains.shape[0]
    h = x
    a = rmsnorm_rows(h, norm_gains[0, 0], BF16)
    for i in range(depth):
        g = norm_gains[i]
        kind, j = i % N_MIXERS, i // N_MIXERS
        h, a = ffn_block(a, h, ffn_w_in[i, 0], ffn_w_out[i, 0], g[1], g[2])
        if kind == 0:
            y = nsa_mixer(a, pos, *[p[j] for p in nsa_params])
        elif kind == 1:
            y = dsa_mixer(a, pos, *[p[j] for p in dsa_params])
        else:
            y = s5_mixer(a, *[p[j] for p in s5_params])
        h, a = residual_norm(h, y, g[3], g[4], 1.0)
        h, a = cross_attn_block(a, h, mem, xa_mem_norm[i], xa_wq[i], xa_wkv[i], xa_wo[i], g[5], g[6])
        g_next = norm_gains[i + 1, 0] if i + 1 < depth else jnp.ones_like(g[0])
        h, a = ffn_block(a, h, ffn_w_in[i, 1], ffn_w_out[i, 1], g[7], g_next)
    return h


def kernel(x, mem, positions, norm_gains, ffn_w_in, ffn_w_out, xa_mem_norm, xa_wq, xa_wkv, xa_wo,
           nsa_w_in, nsa_cmp_pos, nsa_cmp_w1, nsa_cmp_w2, nsa_w_out,
           dsa_w_in, dsa_kv_norm, dsa_w_uk, dsa_w_uv, dsa_w_out,
           s5_lam_re, s5_lam_im, s5_log_step, s5_b_re, s5_b_im, s5_c_re, s5_c_im, s5_d, s5_w_glu):
    nsa_params = (nsa_w_in, nsa_cmp_pos, nsa_cmp_w1, nsa_cmp_w2, nsa_w_out)
    dsa_params = (dsa_w_in, dsa_kv_norm, dsa_w_uk, dsa_w_uv, dsa_w_out)
    s5_params = (s5_lam_re, s5_lam_im, s5_log_step, s5_b_re, s5_b_im, s5_c_re, s5_c_im, s5_d, s5_w_glu)
    outs = [_trunk(x[b], mem[b], positions[b], norm_gains, ffn_w_in, ffn_w_out, xa_mem_norm, xa_wq, xa_wkv, xa_wo,
                   nsa_params, dsa_params, s5_params) for b in range(x.shape[0])]
    return jnp.stack(outs, axis=0)
```
